```python
import math
import jax, jax.numpy as jnp
from jax import lax
import numpy as np

D_MODEL = 1024
BATCH = 8
SEQ = 8192
DEPTH = 1

CHUNK = 64
MIX_WIDTH = D_MODEL
POOL_WIDTH = MIX_WIDTH // 2
ATTN_WIDTH = MIX_WIDTH - POOL_WIDTH
POOL_WINDOWS = (2, 4, 8, 16)
N_POOL_GROUPS = len(POOL_WINDOWS)
POOL_GROUP_DIM = POOL_WIDTH // N_POOL_GROUPS
HEAD_DIM = 64
N_HEADS = ATTN_WIDTH // HEAD_DIM
LEFT_CHUNKS = 8
BAND = (LEFT_CHUNKS + 1) * CHUNK
MAX_REL = 64
N_REL = 2 * MAX_REL + 1
IN_PROJ_WIDTH = 2 * POOL_WIDTH + 4 * ATTN_WIDTH
EPS = 1e-6
MASK_VALUE = -1e30

kernel_name = "hybrid_pool_chunkattn_block"


def rms_norm(x, g):
    xf = x.astype(jnp.float32)
    y = xf * lax.rsqrt(jnp.mean(xf * xf, axis=-1, keepdims=True) + EPS)
    return (y * g.astype(jnp.float32)).astype(x.dtype)


def multiscale_pool(v, pool_w, pool_scale):
    S = v.shape[1]
    vf = v.astype(jnp.float32)
    cs = jnp.pad(jnp.cumsum(vf, axis=1), ((0, 0), (1, 0), (0, 0)))
    t = jnp.arange(S)
    diffs = []
    for gi, w in enumerate(POOL_WINDOWS):
        sl = slice(gi * POOL_GROUP_DIM, (gi + 1) * POOL_GROUP_DIM)
        cs_g = cs[..., sl]
        lo = jnp.maximum(t + 1 - w, 0)
        win_sum = cs_g[:, 1:] - jnp.take(cs_g, lo, axis=1)
        count = (t + 1 - lo).astype(jnp.float32)[None, :, None]
        diffs.append(win_sum / count - vf[..., sl])
    d = jnp.stack(diffs, axis=2)
    y = jnp.einsum('bsgc,gcd->bsgd', d, pool_w.astype(jnp.float32))
    y = y.reshape(y.shape[0], S, POOL_WIDTH) * pool_scale.astype(jnp.float32)
    return y.astype(v.dtype)


def chunked_rel_attention(q, k, v, rel_bias):
    B, S, H, Dh = q.shape
    n_chunks = S // CHUNK
    pad = LEFT_CHUNKS * CHUNK
    kp = jnp.pad(k, ((0, 0), (pad, 0), (0, 0), (0, 0)))
    vp = jnp.pad(v, ((0, 0), (pad, 0), (0, 0), (0, 0)))
    qc = jnp.moveaxis(q.reshape(B, n_chunks, CHUNK, H, Dh), 1, 0)
    i = jnp.arange(CHUNK)
    j = jnp.arange(BAND)
    rel = j[None, :] - pad - i[:, None]
    rel_idx = jnp.clip(rel, -MAX_REL, MAX_REL) + MAX_REL
    bias = rel_bias.astype(jnp.float32)[:, rel_idx]
    scale = 1.0 / math.sqrt(Dh)

    def one_chunk(args):
        c, qb = args
        start = c * CHUNK
        kb = lax.dynamic_slice_in_dim(kp, start, BAND, axis=1)
        vb = lax.dynamic_slice_in_dim(vp, start, BAND, axis=1)
        s = jnp.einsum('bqhd,bkhd->bhqk', qb, kb).astype(jnp.float32) * scale + bias[None]
        valid = (start + j - pad) >= 0
        s = jnp.where(valid[None, None, None, :], s, MASK_VALUE)
        p = jax.nn.softmax(s, axis=-1)
        return jnp.einsum('bhqk,bkhd->bqhd', p.astype(vb.dtype), vb)

    out = lax.map(one_chunk, (jnp.arange(n_chunks), qc))
    return jnp.moveaxis(out, 0, 1).reshape(B, S, H * Dh)


def setup_inputs(seed: int = 0) -> dict:
    key = jax.random.key(seed)
    ks = jax.random.split(key, 9)
    x = jax.random.normal(ks[0], (BATCH, SEQ, D_MODEL), jnp.float32)
    norm_gain = 1.0 + 0.02 * jax.random.normal(ks[1], (DEPTH, D_MODEL), jnp.float32)
    w_in = jax.random.normal(ks[2], (DEPTH, D_MODEL, IN_PROJ_WIDTH), jnp.float32) * D_MODEL ** -0.5
    pool_w = jax.random.normal(ks[3], (DEPTH, N_POOL_GROUPS, POOL_GROUP_DIM, POOL_GROUP_DIM), jnp.float32) * POOL_GROUP_DIM ** -0.5
    pool_scale = 1.0 + 0.02 * jax.random.normal(ks[4], (DEPTH, POOL_WIDTH), jnp.float32)
    rel_bias = 0.5 * jax.random.normal(ks[5], (DEPTH, N_HEADS, N_REL), jnp.float32)
    w_out = jax.random.normal(ks[6], (DEPTH, MIX_WIDTH, D_MODEL), jnp.float32) * MIX_WIDTH ** -0.5
    final_norm_gain = 1.0 + 0.02 * jax.random.normal(ks[7], (D_MODEL,), jnp.float32)
    return {"x": x, "norm_gain": norm_gain, "w_in": w_in, "pool_w": pool_w,
            "pool_scale": pool_scale, "rel_bias": rel_bias, "w_out": w_out,
            "final_norm_gain": final_norm_gain}


def reference(x, norm_gain, w_in, pool_w, pool_scale, rel_bias, w_out, final_norm_gain):
    B, S, _ = x.shape
    for layer in range(DEPTH):
        h = rms_norm(x, norm_gain[layer])
        proj = jnp.einsum('bsd,de->bse', h, w_in[layer])
        o = 0
        pool_v = proj[..., o:o + POOL_WIDTH]; o += POOL_WIDTH
        pool_g = proj[..., o:o + POOL_WIDTH]; o += POOL_WIDTH
        q = proj[..., o:o + ATTN_WIDTH]; o += ATTN_WIDTH
        k = proj[..., o:o + ATTN_WIDTH]; o += ATTN_WIDTH
        v = proj[..., o:o + ATTN_WIDTH]; o += ATTN_WIDTH
        attn_g = proj[..., o:o + ATTN_WIDTH]
        y_pool = multiscale_pool(pool_v, pool_w[layer], pool_scale[layer]) * jax.nn.silu(pool_g)
        qh = q.reshape(B, S, N_HEADS, HEAD_DIM)
        kh = k.reshape(B, S, N_HEADS, HEAD_DIM)
        vh = v.reshape(B, S, N_HEADS, HEAD_DIM)
        y_attn = chunked_rel_attention(qh, kh, vh, rel_bias[layer]) * jax.nn.silu(attn_g)
        y = jnp.concatenate([y_pool, y_attn], axis=-1)
        x = x + jnp.einsum('bse,ed->bsd', y, w_out[layer])
    return rms_norm(x, final_norm_gain)
```

```python
import functools
import math

import jax
import jax.numpy as jnp
from jax import lax
from jax.experimental import pallas as pl
from jax.experimental.pallas import tpu as pltpu

D_MODEL = 1024
CHUNK = 64
POOL_WIDTH = 512
ATTN_WIDTH = 512
POOL_WINDOWS = (2, 4, 8, 16)
POOL_GROUP_DIM = 128
HEAD_DIM = 64
N_HEADS = 8
LEFT_CHUNKS = 8
HALO = LEFT_CHUNKS * CHUNK
BAND = HALO + CHUNK
MAX_REL = 64
N_REL = 2 * MAX_REL + 1
IN_PROJ_WIDTH = 2 * POOL_WIDTH + 4 * ATTN_WIDTH
EPS = 1e-6
MASK_VALUE = -1e30

LANES = 128
HEADS_PER_TILE = LANES // HEAD_DIM
SEQ_TILE = 512
POOL_HALO = 16
VMEM_LIMIT_BYTES = 56 * 1024 * 1024

_F32 = jnp.float32
_BF16 = jnp.bfloat16


def _rms_norm(x, gain):
    return x * lax.rsqrt(jnp.mean(x * x, axis=-1, keepdims=True) + EPS) * gain


def _silu(x):
    return x / (1.0 + jnp.exp(-x))


def _bias_kernel(rel_ref, out_ref):
    h = pl.program_id(0)
    i = lax.broadcasted_iota(jnp.int32, (CHUNK, BAND), 0)
    j = lax.broadcasted_iota(jnp.int32, (CHUNK, BAND), 1)
    idx = jnp.clip(j - HALO - i, -MAX_REL, MAX_REL) + MAX_REL

    def body(m, acc):
        return jnp.where(idx == m, rel_ref[h, m], acc)

    out_ref[0] = lax.fori_loop(0, N_REL, body, jnp.zeros((CHUNK, BAND), _F32))


def _expand_bias(rel_bias):
    return pl.pallas_call(
        _bias_kernel,
        grid=(N_HEADS,),
        in_specs=[pl.BlockSpec(memory_space=pltpu.SMEM)],
        out_specs=pl.BlockSpec((1, CHUNK, BAND), lambda h: (h, 0, 0)),
        out_shape=jax.ShapeDtypeStruct((N_HEADS, CHUNK, BAND), _F32),
        name="rel_bias_expand",
    )(rel_bias)


def _block_kernel(x_ref, g_in_ref, w_in_ref, pool_w_ref, pool_scale_ref, bias_ref, w_out_ref,
                  g_out_ref, out_ref, k_buf, v_buf, q_buf, gate_buf, y_buf, pv_buf):
    t = pl.program_id(1)
    tile = x_ref.shape[1]

    @pl.when(t == 0)
    def _():
        k_buf[0:HALO, :] = jnp.zeros((HALO, ATTN_WIDTH), _BF16)
        v_buf[0:HALO, :] = jnp.zeros((HALO, ATTN_WIDTH), _BF16)
        pv_buf[0:POOL_HALO, :] = jnp.zeros((POOL_HALO, POOL_WIDTH), _F32)

    x = x_ref[0]
    h = _rms_norm(x, g_in_ref[...]).astype(_BF16)

    def proj(part):
        lo = part * POOL_WIDTH
        return jnp.dot(h, w_in_ref[:, lo:lo + POOL_WIDTH], preferred_element_type=_F32)

    pv_buf[POOL_HALO:POOL_HALO + tile, :] = proj(0)
    pool_gate = _silu(proj(1))
    q_buf[...] = (proj(2) * (1.0 / math.sqrt(HEAD_DIM))).astype(_BF16)
    k_buf[HALO:HALO + tile, :] = proj(3).astype(_BF16)
    v_buf[HALO:HALO + tile, :] = proj(4).astype(_BF16)
    gate_buf[...] = _silu(proj(5))

    pos = t * tile + lax.broadcasted_iota(jnp.int32, (tile, 1), 0)
    for g, w in enumerate(POOL_WINDOWS):
        lanes = slice(g * POOL_GROUP_DIM, (g + 1) * POOL_GROUP_DIM)
        cur = pv_buf[POOL_HALO:POOL_HALO + tile, lanes]
        win = cur
        for back in range(1, w):
            win = win + pv_buf[POOL_HALO - back:POOL_HALO - back + tile, lanes]
        count = jnp.minimum(pos + 1, w).astype(_F32)
        d = (win / count - cur).astype(_BF16)
        yg = jnp.dot(d, pool_w_ref[g], preferred_element_type=_F32)
        y_buf[:, lanes] = (yg * pool_scale_ref[:, lanes] * pool_gate[:, lanes]).astype(_BF16)

    lane = lax.broadcasted_iota(jnp.int32, (CHUNK, LANES), 1)
    first_head = lane < HEAD_DIM
    col = lax.broadcasted_iota(jnp.int32, (1, BAND), 1)

    def chunk_body(c, carry):
        r0 = pl.multiple_of(c * CHUNK, CHUNK)
        valid = (t * tile + c * CHUNK + col) >= HALO
        for p in range(N_HEADS // HEADS_PER_TILE):
            lanes = slice(p * LANES, (p + 1) * LANES)
            qp = q_buf[pl.ds(r0, CHUNK), lanes]
            kb = k_buf[pl.ds(r0, BAND), lanes]
            vb = v_buf[pl.ds(r0, BAND), lanes]
            outs = []
            for hh in range(HEADS_PER_TILE):
                sel = first_head if hh == 0 else jnp.logical_not(first_head)
                qm = jnp.where(sel, qp, jnp.zeros_like(qp))
                s = lax.dot_general(qm, kb, (((1,), (1,)), ((), ())), preferred_element_type=_F32)
                s = s + bias_ref[p * HEADS_PER_TILE + hh]
                s = jnp.where(valid, s, MASK_VALUE)
                m = jnp.max(s, axis=-1, keepdims=True)
                e = jnp.exp(s - m)
                l = jnp.sum(e, axis=-1, keepdims=True)
                o = jnp.dot(e.astype(_BF16), vb, preferred_element_type=_F32)
                outs.append(o / l)
            o_pair = jnp.where(first_head, outs[0], outs[1])
            gated = o_pair * gate_buf[pl.ds(r0, CHUNK), lanes]
            y_buf[pl.ds(r0, CHUNK), POOL_WIDTH + p * LANES:POOL_WIDTH + (p + 1) * LANES] = gated.astype(_BF16)
        return carry

    lax.fori_loop(0, tile // CHUNK, chunk_body, 0)

    k_buf[0:HALO, :] = k_buf[tile:tile + HALO, :]
    v_buf[0:HALO, :] = v_buf[tile:tile + HALO, :]
    pv_buf[0:POOL_HALO, :] = pv_buf[tile:tile + POOL_HALO, :]

    res = x + jnp.dot(y_buf[...], w_out_ref[...], preferred_element_type=_F32)
    out_ref[0] = _rms_norm(res, g_out_ref[...])


def _fused_block(x, g_in, w_in, pool_w, pool_scale, bias, w_out, g_out, *, tile):
    batch, seq, d = x.shape
    assert d == D_MODEL and seq % tile == 0 and tile % CHUNK == 0 and tile >= HALO
    const2 = lambda b, t: (0, 0)
    const3 = lambda b, t: (0, 0, 0)
    return pl.pallas_call(
        _block_kernel,
        grid=(batch, seq // tile),
        in_specs=[
            pl.BlockSpec((1, tile, D_MODEL), lambda b, t: (b, t, 0)),
            pl.BlockSpec((1, D_MODEL), const2),
            pl.BlockSpec((D_MODEL, IN_PROJ_WIDTH), const2),
            pl.BlockSpec((len(POOL_WINDOWS), POOL_GROUP_DIM, POOL_GROUP_DIM), const3),
            pl.BlockSpec((1, POOL_WIDTH), const2),
            pl.BlockSpec((N_HEADS, CHUNK, BAND), const3),
            pl.BlockSpec((POOL_WIDTH + ATTN_WIDTH, D_MODEL), const2),
            pl.BlockSpec((1, D_MODEL), const2),
        ],
        out_specs=pl.BlockSpec((1, tile, D_MODEL), lambda b, t: (b, t, 0)),
        out_shape=jax.ShapeDtypeStruct(x.shape, x.dtype),
        scratch_shapes=[
            pltpu.VMEM((HALO + tile, ATTN_WIDTH), _BF16),
            pltpu.VMEM((HALO + tile, ATTN_WIDTH), _BF16),
            pltpu.VMEM((tile, ATTN_WIDTH), _BF16),
            pltpu.VMEM((tile, ATTN_WIDTH), _F32),
            pltpu.VMEM((tile, POOL_WIDTH + ATTN_WIDTH), _BF16),
            pltpu.VMEM((POOL_HALO + tile, POOL_WIDTH), _F32),
        ],
        compiler_params=pltpu.CompilerParams(
            dimension_semantics=("arbitrary", "arbitrary"),
            vmem_limit_bytes=VMEM_LIMIT_BYTES,
        ),
        name="hybrid_block",
    )(x, g_in, w_in, pool_w, pool_scale, bias, w_out, g_out)


@jax.jit
def kernel(x, norm_gain, w_in, pool_w, pool_scale, rel_bias, w_out, final_norm_gain):
    bias = _expand_bias(rel_bias[0])
    return _fused_block(
        x,
        norm_gain[0][None, :],
        w_in[0].astype(_BF16),
        pool_w[0].astype(_BF16),
        pool_scale[0][None, :],
        bias,
        w_out[0].astype(_BF16),
        final_norm_gain[None, :],
        tile=SEQ_TILE,
    )
```

```python
import math

import jax
import jax.numpy as jnp
from jax import lax
from jax.experimental import pallas as pl
from jax.experimental.pallas import tpu as pltpu

D_MODEL = 1024
CHUNK = 64
POOL_WIDTH = 512
ATTN_WIDTH = 512
POOL_WINDOWS = (2, 4, 8, 16)
POOL_GROUP_DIM = 128
HEAD_DIM = 64
N_HEADS = 8
LEFT_CHUNKS = 8
HALO = LEFT_CHUNKS * CHUNK
MAX_REL = 64
N_REL = 2 * MAX_REL + 1
EPS = 1e-6
MASK_VALUE = -1e30

LANES = 128
HEADS_PER_TILE = LANES // HEAD_DIM
N_PAIRS = N_HEADS // HEADS_PER_TILE
GROUP = LANES
GROUP_BAND = HALO + GROUP
SEQ_TILE = 512
POOL_HALO = 16
VMEM_LIMIT_BYTES = 56 * 1024 * 1024

_F32 = jnp.float32
_BF16 = jnp.bfloat16
_NT = (((1,), (1,)), ((), ()))


def _rms_norm(x, gain):
    return x * lax.rsqrt(jnp.mean(x * x, axis=-1, keepdims=True) + EPS) * gain


def _silu(x):
    return x / (1.0 + jnp.exp(-x))


def _bias_kernel(rel_ref, out_ref):
    p = pl.program_id(0)
    j = lax.broadcasted_iota(jnp.int32, (GROUP_BAND, GROUP), 0)
    i = lax.broadcasted_iota(jnp.int32, (GROUP_BAND, GROUP), 1)
    idx = jnp.clip(j - HALO - i, -MAX_REL, MAX_REL) + MAX_REL
    for hh in range(HEADS_PER_TILE):
        head = p * HEADS_PER_TILE + hh

        def body(m, acc):
            return jnp.where(idx == m, rel_ref[head, m], acc)

        out_ref[0, :, hh * GROUP:(hh + 1) * GROUP] = lax.fori_loop(
            0, N_REL, body, jnp.zeros((GROUP_BAND, GROUP), _F32))


def _expand_bias(rel_bias):
    shape = (N_PAIRS, GROUP_BAND, HEADS_PER_TILE * GROUP)
    return pl.pallas_call(
        _bias_kernel,
        grid=(N_PAIRS,),
        in_specs=[pl.BlockSpec(memory_space=pltpu.SMEM)],
        out_specs=pl.BlockSpec((1,) + shape[1:], lambda p: (p, 0, 0)),
        out_shape=jax.ShapeDtypeStruct(shape, _F32),
        name="rel_bias_expand",
    )(rel_bias)


def _attend_group(g, band_lo, k_buf, vt_buf, qt_buf, gate_buf, bias_ref, y_buf, s_buf):
    nk = GROUP_BAND - band_lo
    q0 = g * GROUP
    row = lax.broadcasted_iota(jnp.int32, (LANES, GROUP), 0)
    first_rows = row < HEAD_DIM
    lane = lax.broadcasted_iota(jnp.int32, (CHUNK, HEADS_PER_TILE * GROUP), 1)
    second_chunk = (lane & (GROUP - 1)) >= CHUNK

    def scores(p):
        lanes = slice(p * LANES, (p + 1) * LANES)
        kb = k_buf[pl.ds(q0 + band_lo, nk), lanes]
        qt = qt_buf[g, lanes, :]
        zero = jnp.zeros_like(qt)
        rhs = jnp.concatenate([jnp.where(first_rows, qt, zero), jnp.where(first_rows, zero, qt)], axis=1)
        s_buf[p % 2, 0:nk, :] = jnp.dot(kb, rhs, preferred_element_type=_F32)

    def finish(p):
        lanes = slice(p * LANES, (p + 1) * LANES)
        s = s_buf[p % 2, 0:nk, :] + bias_ref[p, band_lo:, :]
        blocks = [s[:nk - CHUNK], jnp.where(second_chunk, s[nk - CHUNK:], MASK_VALUE)]
        if band_lo == 0:
            blocks = [jnp.where(second_chunk, MASK_VALUE, s[:CHUNK]), blocks[0][CHUNK:], blocks[1]]
        s = jnp.concatenate(blocks, axis=0)
        m = jnp.max(s, axis=0, keepdims=True)
        e = jnp.exp(s - m)
        l = jnp.sum(e, axis=0, keepdims=True)
        vt = jnp.concatenate(
            [vt_buf[g + (band_lo // LANES) + b, lanes, :] for b in range(nk // LANES)], axis=1)
        o = jnp.dot(vt, e.astype(_BF16), preferred_element_type=_F32)
        inv_l = 1.0 / l
        ot = jnp.concatenate([o[:HEAD_DIM, :GROUP] * inv_l[:, :GROUP],
                              o[HEAD_DIM:, GROUP:] * inv_l[:, GROUP:]], axis=0)
        gated = ot.T * gate_buf[pl.ds(q0, GROUP), lanes]
        y_buf[pl.ds(q0, GROUP), POOL_WIDTH + p * LANES:POOL_WIDTH + (p + 1) * LANES] = gated.astype(_BF16)

    scores(0)
    for p in range(N_PAIRS):
        if p + 1 < N_PAIRS:
            scores(p + 1)
        finish(p)


def _block_kernel(x_ref, g_in_ref, w_main_ref, w_qvt_ref, pool_w_ref, pool_scale_ref, bias_ref,
                  w_out_ref, g_out_ref, out_ref, k_buf, vt_buf, qt_buf, gate_buf, y_buf, pv_buf, s_buf):
    t = pl.program_id(1)
    tile = x_ref.shape[1]
    n_groups = tile // GROUP
    halo_blocks = HALO // LANES

    @pl.when(t == 0)
    def _():
        pv_buf[0:POOL_HALO, :] = jnp.zeros((POOL_HALO, POOL_WIDTH), _F32)

    x = x_ref[0]
    h = _rms_norm(x, g_in_ref[...]).astype(_BF16)

    def proj(part):
        lo = part * POOL_WIDTH
        return jnp.dot(h, w_main_ref[:, lo:lo + POOL_WIDTH], preferred_element_type=_F32)

    pv_buf[POOL_HALO:POOL_HALO + tile, :] = proj(0)
    pool_gate = _silu(proj(1))
    k_buf[HALO:HALO + tile, :] = proj(2).astype(_BF16)
    gate_buf[...] = _silu(proj(3))
    qt = lax.dot_general(w_qvt_ref[0:ATTN_WIDTH, :], h, _NT, preferred_element_type=_F32)
    vt = lax.dot_general(w_qvt_ref[ATTN_WIDTH:, :], h, _NT, preferred_element_type=_F32)
    for g in range(n_groups):
        cols = slice(g * GROUP, (g + 1) * GROUP)
        qt_buf[g] = (qt[:, cols] * (1.0 / math.sqrt(HEAD_DIM))).astype(_BF16)
        vt_buf[halo_blocks + g] = vt[:, cols].astype(_BF16)

    pos = t * tile + lax.broadcasted_iota(jnp.int32, (tile, 1), 0)
    for g, w in enumerate(POOL_WINDOWS):
        lanes = slice(g * POOL_GROUP_DIM, (g + 1) * POOL_GROUP_DIM)
        cur = pv_buf[POOL_HALO:POOL_HALO + tile, lanes]
        win = cur
        for back in range(1, w):
            win = win + pv_buf[POOL_HALO - back:POOL_HALO - back + tile, lanes]
        count = jnp.minimum(pos + 1, w).astype(_F32)
        d = (win / count - cur).astype(_BF16)
        yg = jnp.dot(d, pool_w_ref[g], preferred_element_type=_F32)
        y_buf[:, lanes] = (yg * pool_scale_ref[:, lanes] * pool_gate[:, lanes]).astype(_BF16)

    bufs = (k_buf, vt_buf, qt_buf, gate_buf, bias_ref, y_buf, s_buf)

    @pl.when(t == 0)
    def _():
        for g in range(n_groups):
            _attend_group(g, max(HALO - g * GROUP, 0), *bufs)

    @pl.when(t != 0)
    def _():
        def group_body(g, carry):
            _attend_group(g, 0, *bufs)
            return carry
        lax.fori_loop(0, n_groups, group_body, 0)

    k_buf[0:HALO, :] = k_buf[tile:tile + HALO, :]
    for b in range(halo_blocks):
        vt_buf[b] = vt_buf[n_groups + b]
    pv_buf[0:POOL_HALO, :] = pv_buf[tile:tile + POOL_HALO, :]

    res = x + jnp.dot(y_buf[...], w_out_ref[...], preferred_element_type=_F32)
    out_ref[0] = _rms_norm(res, g_out_ref[...])


def _fused_block(x, g_in, w_main, w_qvt, pool_w, pool_scale, bias, w_out, g_out, *, tile):
    batch, seq, d = x.shape
    assert d == D_MODEL and seq % tile == 0 and tile % GROUP == 0 and tile >= HALO
    const2 = lambda b, t: (0, 0)
    const3 = lambda b, t: (0, 0, 0)
    return pl.pallas_call(
        _block_kernel,
        grid=(batch, seq // tile),
        in_specs=[
            pl.BlockSpec((1, tile, D_MODEL), lambda b, t: (b, t, 0)),
            pl.BlockSpec((1, D_MODEL), const2),
            pl.BlockSpec(w_main.shape, const2),
            pl.BlockSpec(w_qvt.shape, const2),
            pl.BlockSpec(pool_w.shape, const3),
            pl.BlockSpec((1, POOL_WIDTH), const2),
            pl.BlockSpec(bias.shape, const3),
            pl.BlockSpec(w_out.shape, const2),
            pl.BlockSpec((1, D_MODEL), const2),
        ],
        out_specs=pl.BlockSpec((1, tile, D_MODEL), lambda b, t: (b, t, 0)),
        out_shape=jax.ShapeDtypeStruct(x.shape, x.dtype),
        scratch_shapes=[
            pltpu.VMEM((HALO + tile, ATTN_WIDTH), _BF16),
            pltpu.VMEM(((HALO + tile) // LANES, ATTN_WIDTH, LANES), _BF16),
            pltpu.VMEM((tile // GROUP, ATTN_WIDTH, GROUP), _BF16),
            pltpu.VMEM((tile, ATTN_WIDTH), _F32),
            pltpu.VMEM((tile, POOL_WIDTH + ATTN_WIDTH), _BF16),
            pltpu.VMEM((POOL_HALO + tile, POOL_WIDTH), _F32),
            pltpu.VMEM((2, GROUP_BAND, HEADS_PER_TILE * GROUP), _F32),
        ],
        compiler_params=pltpu.CompilerParams(
            dimension_semantics=("arbitrary", "arbitrary"),
            vmem_limit_bytes=VMEM_LIMIT_BYTES,
        ),
        name="hybrid_block",
    )(x, g_in, w_main, w_qvt, pool_w, pool_scale, bias, w_out, g_out)


@jax.jit
def kernel(x, norm_gain, w_in, pool_w, pool_scale, rel_bias, w_out, final_norm_gain):
    w = w_in[0]
    pool_v, pool_g, wq, wk, wv, attn_g = (w[:, i * POOL_WIDTH:(i + 1) * POOL_WIDTH] for i in range(6))
    w_main = jnp.concatenate([pool_v, pool_g, wk, attn_g], axis=1).astype(_BF16)
    w_qvt = jnp.concatenate([wq, wv], axis=1).T.astype(_BF16)
    return _fused_block(
        x,
        norm_gain[0][None, :],
        w_main,
        w_qvt,
        pool_w[0].astype(_BF16),
        pool_scale[0][None, :],
        _expand_bias(rel_bias[0]),
        w_out[0].astype(_BF16),
        final_norm_gain[None, :],
        tile=SEQ_TILE,
    )
```

```python
import math

import jax
import jax.numpy as jnp
from jax import lax
from jax.experimental import pallas as pl
from jax.experimental.pallas import tpu as pltpu

D_MODEL = 1024
CHUNK = 64
POOL_WIDTH = 512
ATTN_WIDTH = 512
POOL_WINDOWS = (2, 4, 8, 16)
POOL_GROUP_DIM = 128
HEAD_DIM = 64
N_HEADS = 8
LEFT_CHUNKS = 8
HALO = LEFT_CHUNKS * CHUNK
MAX_REL = 64
N_REL = 2 * MAX_REL + 1
EPS = 1e-6
MASK_VALUE = -1e30
LOG2_E = math.log2(math.e)
Q_SCALE = LOG2_E / math.sqrt(HEAD_DIM)

LANES = 128
HEADS_PER_TILE = LANES // HEAD_DIM
N_PAIRS = N_HEADS // HEADS_PER_TILE
GROUP = LANES
GROUP_BAND = HALO + GROUP
SCORE_COLS = HEADS_PER_TILE * GROUP
BIAS_ROWS = GROUP + MAX_REL
SEQ_TILE = 512
POOL_HALO = 16
OUT_COLS = 256
VMEM_LIMIT_BYTES = 56 * 1024 * 1024

_F32 = jnp.float32
_BF16 = jnp.bfloat16
_NT = (((1,), (1,)), ((), ()))


def _rms_norm(x, gain):
    return x * lax.rsqrt(jnp.mean(x * x, axis=-1, keepdims=True) + EPS) * gain


def _silu(x):
    return x / (1.0 + jnp.exp(-x))


def _bias_kernel(rel_ref, out_ref):
    p = pl.program_id(0)
    j = lax.broadcasted_iota(jnp.int32, (BIAS_ROWS, GROUP), 0) + (GROUP_BAND - BIAS_ROWS)
    i = lax.broadcasted_iota(jnp.int32, (BIAS_ROWS, GROUP), 1)
    idx = jnp.clip(j - HALO - i, -MAX_REL, MAX_REL) + MAX_REL
    for hh in range(HEADS_PER_TILE):
        head = p * HEADS_PER_TILE + hh

        def body(m, acc):
            return jnp.where(idx == m, rel_ref[head, m], acc)

        table = lax.fori_loop(0, N_REL, body, jnp.zeros((BIAS_ROWS, GROUP), _F32))
        out_ref[0, :, hh * GROUP:(hh + 1) * GROUP] = (table - rel_ref[head, 0]) * LOG2_E


def _expand_bias(rel_bias):
    shape = (N_PAIRS, BIAS_ROWS, SCORE_COLS)
    return pl.pallas_call(
        _bias_kernel,
        grid=(N_PAIRS,),
        in_specs=[pl.BlockSpec(memory_space=pltpu.SMEM)],
        out_specs=pl.BlockSpec((1,) + shape[1:], lambda p: (p, 0, 0)),
        out_shape=jax.ShapeDtypeStruct(shape, _F32),
        name="rel_bias_expand",
    )(rel_bias)


def _attend(groups, fillers, k_buf, ind_buf, vt_buf, qt_buf, gate_buf, bias_ref, y_buf, s_buf):
    row = lax.broadcasted_iota(jnp.int32, (LANES, GROUP), 0)
    first_rows = row < HEAD_DIM
    lane = lax.broadcasted_iota(jnp.int32, (CHUNK, SCORE_COLS), 1)
    second_chunk = (lane & (GROUP - 1)) >= CHUNK
    pen_row = lax.broadcasted_iota(jnp.int32, (LANES, SCORE_COLS), 0)
    pen = jnp.where(pen_row == 0, MASK_VALUE, 0.0).astype(_BF16)
    items = [(g, p) for g in groups for p in range(N_PAIRS)]
    assert len(fillers) == len(items)

    def scores(i):
        g, p = items[i]
        lanes = slice(p * LANES, (p + 1) * LANES)
        band = slice(g * GROUP, g * GROUP + GROUP_BAND)
        lhs = jnp.concatenate([k_buf[band, lanes], ind_buf[band, :]], axis=1)
        qt = qt_buf[g, lanes, :]
        zero = jnp.zeros_like(qt)
        rhs = jnp.concatenate([jnp.where(first_rows, qt, zero), jnp.where(first_rows, zero, qt)], axis=1)
        rhs = jnp.concatenate([rhs, pen], axis=0)
        mid = GROUP_BAND // 2
        s_buf[i % 2, 0:mid, :] = jnp.dot(lhs[:mid], rhs, preferred_element_type=_F32)
        s_buf[i % 2, mid:, :] = jnp.dot(lhs[mid:], rhs, preferred_element_type=_F32)

    def finish(i):
        g, p = items[i]
        lanes = slice(p * LANES, (p + 1) * LANES)
        rows = slice(g * GROUP, (g + 1) * GROUP)
        s = s_buf[i % 2]
        bias = bias_ref[p]
        lo = GROUP_BAND - BIAS_ROWS
        s = jnp.concatenate([
            jnp.where(second_chunk, MASK_VALUE, s[:CHUNK]),
            s[CHUNK:lo],
            s[lo:GROUP_BAND - CHUNK] + bias[:BIAS_ROWS - CHUNK],
            jnp.where(second_chunk, s[GROUP_BAND - CHUNK:] + bias[BIAS_ROWS - CHUNK:], MASK_VALUE),
        ], axis=0)
        m = jnp.max(s, axis=0, keepdims=True)
        e = jnp.exp2(s - m)
        l = jnp.sum(e, axis=0, keepdims=True)
        vt = jnp.concatenate([vt_buf[g + b, lanes, :] for b in range(GROUP_BAND // LANES)], axis=1)
        eb = e.astype(_BF16)
        inv_l = 1.0 / l
        ot = jnp.concatenate([
            jnp.dot(vt[:HEAD_DIM], eb[:, :GROUP], preferred_element_type=_F32) * inv_l[:, :GROUP],
            jnp.dot(vt[HEAD_DIM:], eb[:, GROUP:], preferred_element_type=_F32) * inv_l[:, GROUP:],
        ], axis=0)
        gated = ot.T * gate_buf[rows, lanes]
        y_buf[rows, POOL_WIDTH + p * LANES:POOL_WIDTH + (p + 1) * LANES] = gated.astype(_BF16)

    scores(0)
    for i in range(len(items)):
        for filler in fillers[i]:
            filler()
        if i + 1 < len(items):
            scores(i + 1)
        finish(i)


def _spread(pieces, n):
    return [pieces[(i * len(pieces)) // n:((i + 1) * len(pieces)) // n] for i in range(n)]


def _block_kernel(x_ref, g_in_ref, w_main_ref, w_qvt_ref, pool_w_ref, pool_scale_ref, bias_ref,
                  w_out_ref, g_out_ref, out_ref,
                  h_buf, k_buf, ind_buf, vt_buf, qt_buf, gate_buf, pgate_buf, y_buf, pv_buf, s_buf):
    t = pl.program_id(1)
    tile = x_ref.shape[1]
    half = tile // 2
    n_groups = tile // GROUP
    halo_blocks = HALO // LANES

    @pl.when(t == 0)
    def _():
        pv_buf[0:POOL_HALO, :] = jnp.zeros((POOL_HALO, POOL_WIDTH), _F32)
        k_buf[0:HALO, :] = jnp.zeros((HALO, ATTN_WIDTH), _BF16)
        for b in range(halo_blocks):
            vt_buf[b] = jnp.zeros((ATTN_WIDTH, LANES), _BF16)
        lane = lax.broadcasted_iota(jnp.int32, (HALO, LANES), 1)
        ind_buf[0:HALO, :] = jnp.where(lane == 0, 1.0, 0.0).astype(_BF16)
        ind_buf[HALO:, :] = jnp.zeros((tile, LANES), _BF16)

    h_buf[...] = _rms_norm(x_ref[0], g_in_ref[...]).astype(_BF16)

    def proj_pieces(hf):
        rows = slice(hf * half, (hf + 1) * half)

        def main(part):
            lo = part * POOL_WIDTH
            return jnp.dot(h_buf[rows, :], w_main_ref[:, lo:lo + POOL_WIDTH], preferred_element_type=_F32)

        def pool_v():
            pv_buf[POOL_HALO + hf * half:POOL_HALO + (hf + 1) * half, :] = main(0)

        def pool_g():
            pgate_buf[rows, :] = _silu(main(1))

        def keys():
            k_buf[HALO + hf * half:HALO + (hf + 1) * half, :] = main(2).astype(_BF16)

        def attn_g():
            gate_buf[rows, :] = _silu(main(3))

        def transposed(part):
            w = w_qvt_ref[part * ATTN_WIDTH:(part + 1) * ATTN_WIDTH, :]
            return lax.dot_general(w, h_buf[rows, :], _NT, preferred_element_type=_F32)

        def queries():
            qt = transposed(0)
            for gg in range(half // GROUP):
                qt_buf[hf * (half // GROUP) + gg] = (qt[:, gg * GROUP:(gg + 1) * GROUP] * Q_SCALE).astype(_BF16)

        def values():
            vt = transposed(1)
            for gg in range(half // GROUP):
                vt_buf[halo_blocks + hf * (half // GROUP) + gg] = vt[:, gg * GROUP:(gg + 1) * GROUP].astype(_BF16)

        def pool(g):
            def run():
                w = POOL_WINDOWS[g]
                lanes = slice(g * POOL_GROUP_DIM, (g + 1) * POOL_GROUP_DIM)
                base = hf * half
                win = pv_buf[base:base + POOL_HALO + half, lanes]
                shift = 1
                while shift < w:
                    win = win + pltpu.roll(win, shift, 0)
                    shift *= 2
                win = win[POOL_HALO:]
                cur = pv_buf[POOL_HALO + base:POOL_HALO + base + half, lanes]
                head = POOL_HALO
                pos = t * tile + base + lax.broadcasted_iota(jnp.int32, (head, 1), 0)
                count = jnp.minimum(pos + 1, w).astype(_F32)
                mean = jnp.concatenate([win[:head] / count, win[head:] * (1.0 / w)], axis=0)
                d = (mean - cur).astype(_BF16)
                yg = jnp.dot(d, pool_w_ref[g], preferred_element_type=_F32)
                y_buf[rows, lanes] = (yg * pool_scale_ref[:, lanes] * pgate_buf[rows, lanes]).astype(_BF16)
            return run

        return [keys, queries, values, attn_g, pool_v, pool_g] + [pool(g) for g in range(len(POOL_WINDOWS))]

    def out_pieces(hf):
        rows = slice(hf * half, (hf + 1) * half)

        def project(c):
            def run():
                cols = slice(c * OUT_COLS, (c + 1) * OUT_COLS)
                out_ref[0, rows, cols] = x_ref[0, rows, cols] + jnp.dot(
                    y_buf[rows, :], w_out_ref[:, cols], preferred_element_type=_F32)
            return run

        def norm():
            out_ref[0, rows, :] = _rms_norm(out_ref[0, rows, :], g_out_ref[...])

        return [project(c) for c in range(D_MODEL // OUT_COLS)] + [norm]

    bufs = (k_buf, ind_buf, vt_buf, qt_buf, gate_buf, bias_ref, y_buf, s_buf)
    groups_per_half = half // GROUP
    items_per_half = groups_per_half * N_PAIRS

    for piece in proj_pieces(0):
        piece()
    _attend(list(range(n_groups)),
            _spread(proj_pieces(1), items_per_half) + _spread(out_pieces(0), items_per_half), *bufs)
    for piece in out_pieces(1):
        piece()

    k_buf[0:HALO, :] = k_buf[tile:tile + HALO, :]
    for b in range(halo_blocks):
        vt_buf[b] = vt_buf[n_groups + b]
    pv_buf[0:POOL_HALO, :] = pv_buf[tile:tile + POOL_HALO, :]
    ind_buf[0:HALO, :] = jnp.zeros((HALO, LANES), _BF16)


def _fused_block(x, g_in, w_main, w_qvt, pool_w, pool_scale, bias, w_out, g_out, *, tile):
    batch, seq, d = x.shape
    assert d == D_MODEL and seq % tile == 0 and tile % (2 * GROUP) == 0 and tile >= HALO
    const2 = lambda b, t: (0, 0)
    const3 = lambda b, t: (0, 0, 0)
    return pl.pallas_call(
        _block_kernel,
        grid=(batch, seq // tile),
        in_specs=[
            pl.BlockSpec((1, tile, D_MODEL), lambda b, t: (b, t, 0)),
            pl.BlockSpec((1, D_MODEL), const2),
            pl.BlockSpec(w_main.shape, const2),
            pl.BlockSpec(w_qvt.shape, const2),
            pl.BlockSpec(pool_w.shape, const3),
            pl.BlockSpec((1, POOL_WIDTH), const2),
            pl.BlockSpec(bias.shape, const3),
            pl.BlockSpec(w_out.shape, const2),
            pl.BlockSpec((1, D_MODEL), const2),
        ],
        out_specs=pl.BlockSpec((1, tile, D_MODEL), lambda b, t: (b, t, 0)),
        out_shape=jax.ShapeDtypeStruct(x.shape, x.dtype),
        scratch_shapes=[
            pltpu.VMEM((tile, D_MODEL), _BF16),
            pltpu.VMEM((HALO + tile, ATTN_WIDTH), _BF16),
            pltpu.VMEM((HALO + tile, LANES), _BF16),
            pltpu.VMEM(((HALO + tile) // LANES, ATTN_WIDTH, LANES), _BF16),
            pltpu.VMEM((tile // GROUP, ATTN_WIDTH, GROUP), _BF16),
            pltpu.VMEM((tile, ATTN_WIDTH), _F32),
            pltpu.VMEM((tile, POOL_WIDTH), _F32),
            pltpu.VMEM((tile, POOL_WIDTH + ATTN_WIDTH), _BF16),
            pltpu.VMEM((POOL_HALO + tile, POOL_WIDTH), _F32),
            pltpu.VMEM((2, GROUP_BAND, SCORE_COLS), _F32),
        ],
        compiler_params=pltpu.CompilerParams(
            dimension_semantics=("arbitrary", "arbitrary"),
            vmem_limit_bytes=VMEM_LIMIT_BYTES,
        ),
        name="hybrid_block",
    )(x, g_in, w_main, w_qvt, pool_w, pool_scale, bias, w_out, g_out)


@jax.jit
def kernel(x, norm_gain, w_in, pool_w, pool_scale, rel_bias, w_out, final_norm_gain):
    w = w_in[0]
    pool_v, pool_g, wq, wk, wv, attn_g = (w[:, i * POOL_WIDTH:(i + 1) * POOL_WIDTH] for i in range(6))
    w_main = jnp.concatenate([pool_v, pool_g, wk, attn_g], axis=1).astype(_BF16)
    w_qvt = jnp.concatenate([wq, wv], axis=1).T.astype(_BF16)
    return _fused_block(
        x,
        norm_gain[0][None, :],
        w_main,
        w_qvt,
        pool_w[0].astype(_BF16),
        pool_scale[0][None, :],
        _expand_bias(rel_bias[0]),
        w_out[0].astype(_BF16),
        final_norm_gain[None, :],
        tile=SEQ_TILE,
    )
```

```python
import math

import jax
import jax.numpy as jnp
from jax import lax
from jax.experimental import pallas as pl
from jax.experimental.pallas import tpu as pltpu

D_MODEL = 1024
CHUNK = 64
POOL_WIDTH = 512
ATTN_WIDTH = 512
POOL_WINDOWS = (2, 4, 8, 16)
POOL_GROUP_DIM = 128
HEAD_DIM = 64
N_HEADS = 8
LEFT_CHUNKS = 8
HALO = LEFT_CHUNKS * CHUNK
MAX_REL = 64
N_REL = 2 * MAX_REL + 1
EPS = 1e-6
MASK_VALUE = -1e30
LOG2_E = math.log2(math.e)
Q_SCALE = LOG2_E / math.sqrt(HEAD_DIM)

LANES = 128
HEADS_PER_TILE = LANES // HEAD_DIM
N_PAIRS = N_HEADS // HEADS_PER_TILE
GROUP = LANES
GROUP_BAND = HALO + GROUP
SCORE_COLS = HEADS_PER_TILE * GROUP
BIAS_ROWS = GROUP + MAX_REL
SEQ_TILE = 512
POOL_HALO = 16
OUT_COLS = 256
N_ATTN_PIECES = 4
LOOKAHEAD = 3
VMEM_LIMIT_BYTES = 56 * 1024 * 1024

_F32 = jnp.float32
_BF16 = jnp.bfloat16
_NT = (((1,), (1,)), ((), ()))


def _rms_norm(x, gain):
    return x * lax.rsqrt(jnp.mean(x * x, axis=-1, keepdims=True) + EPS) * gain


def _silu(x):
    return x / (1.0 + jnp.exp(-x))


def _bias_kernel(rel_lo_ref, rel_hi_ref, rel_0_ref, out_ref):
    j = lax.broadcasted_iota(jnp.int32, (BIAS_ROWS, GROUP), 0) + (GROUP_BAND - BIAS_ROWS)
    i = lax.broadcasted_iota(jnp.int32, (BIAS_ROWS, GROUP), 1)
    idx = jnp.clip(j - HALO - i, -MAX_REL, MAX_REL) + MAX_REL
    low = jnp.minimum(idx, LANES - 1)
    for head in range(N_HEADS):
        p, hh = divmod(head, HEADS_PER_TILE)
        row = jnp.broadcast_to(rel_lo_ref[head:head + 1, :], (BIAS_ROWS, LANES))
        table = jnp.take_along_axis(row, low, axis=1)
        table = jnp.where(idx == LANES, rel_hi_ref[head:head + 1, :], table)
        out_ref[p, :, hh * GROUP:(hh + 1) * GROUP] = (table - rel_0_ref[head:head + 1, :]) * LOG2_E


def _expand_bias(rel_bias):
    assert N_REL == LANES + 1
    rel_lo = rel_bias[:, :LANES]
    rel_hi = jnp.broadcast_to(rel_bias[:, LANES:], (N_HEADS, LANES))
    rel_0 = jnp.broadcast_to(rel_bias[:, :1], (N_HEADS, LANES))
    return pl.pallas_call(
        _bias_kernel,
        out_shape=jax.ShapeDtypeStruct((N_PAIRS, BIAS_ROWS, SCORE_COLS), _F32),
        name="rel_bias_expand",
    )(rel_lo, rel_hi, rel_0)


def _attend(groups, fillers, k_buf, ind_buf, vt_buf, qt_buf, gate_buf, bias_ref, y_buf, s_buf, m_buf):
    row = lax.broadcasted_iota(jnp.int32, (LANES, GROUP), 0)
    first_rows = row < HEAD_DIM
    lane = lax.broadcasted_iota(jnp.int32, (CHUNK, SCORE_COLS), 1)
    second_chunk = (lane & (GROUP - 1)) >= CHUNK
    pen_row = lax.broadcasted_iota(jnp.int32, (LANES, SCORE_COLS), 0)
    pen = jnp.where(pen_row == 0, MASK_VALUE, 0.0).astype(_BF16)
    items = [(g, p) for g in groups for p in range(N_PAIRS)]
    assert len(fillers) == len(items)

    def scores(i):
        g, p = items[i]
        lanes = slice(p * LANES, (p + 1) * LANES)
        band = slice(g * GROUP, g * GROUP + GROUP_BAND)
        lhs = jnp.concatenate([k_buf[band, lanes], ind_buf[band, :]], axis=1)
        qt = qt_buf[g, lanes, :]
        zero = jnp.zeros_like(qt)
        rhs = jnp.concatenate([jnp.where(first_rows, qt, zero), jnp.where(first_rows, zero, qt)], axis=1)
        rhs = jnp.concatenate([rhs, pen], axis=0)
        mid = GROUP_BAND // 2
        lo = GROUP_BAND - BIAS_ROWS
        bias = bias_ref[p]
        s0 = jnp.dot(lhs[:mid], rhs, preferred_element_type=_F32)
        s1 = jnp.dot(lhs[mid:], rhs, preferred_element_type=_F32)
        s0 = jnp.concatenate([jnp.where(second_chunk, MASK_VALUE, s0[:CHUNK]), s0[CHUNK:]], axis=0)
        s1 = jnp.concatenate([
            s1[:lo - mid],
            s1[lo - mid:-CHUNK] + bias[:BIAS_ROWS - CHUNK],
            jnp.where(second_chunk, s1[-CHUNK:] + bias[BIAS_ROWS - CHUNK:], MASK_VALUE),
        ], axis=0)
        slot = i % (LOOKAHEAD + 1)
        s_buf[slot, 0:mid, :] = s0
        s_buf[slot, mid:, :] = s1
        m_buf[slot] = jnp.maximum(jnp.max(s0, axis=0, keepdims=True), jnp.max(s1, axis=0, keepdims=True))

    def finish(i):
        g, p = items[i]
        lanes = slice(p * LANES, (p + 1) * LANES)
        rows = slice(g * GROUP, (g + 1) * GROUP)
        slot = i % (LOOKAHEAD + 1)
        e = jnp.exp2(s_buf[slot] - m_buf[slot])
        l = jnp.sum(e, axis=0, keepdims=True)
        vt = jnp.concatenate([vt_buf[g + b, lanes, :] for b in range(GROUP_BAND // LANES)], axis=1)
        eb = e.astype(_BF16)
        inv_l = 1.0 / l
        ot = jnp.concatenate([
            jnp.dot(vt[:HEAD_DIM], eb[:, :GROUP], preferred_element_type=_F32) * inv_l[:, :GROUP],
            jnp.dot(vt[HEAD_DIM:], eb[:, GROUP:], preferred_element_type=_F32) * inv_l[:, GROUP:],
        ], axis=0)
        gated = ot.T * gate_buf[rows, lanes]
        y_buf[rows, POOL_WIDTH + p * LANES:POOL_WIDTH + (p + 1) * LANES] = gated.astype(_BF16)

    for i in range(min(LOOKAHEAD, len(items))):
        scores(i)
    for i in range(len(items)):
        for filler in fillers[i]:
            filler()
        if i + LOOKAHEAD < len(items):
            scores(i + LOOKAHEAD)
        finish(i)


def _spread(pieces, n):
    return [pieces[(i * len(pieces)) // n:((i + 1) * len(pieces)) // n] for i in range(n)]


def _block_kernel(x_ref, g_in_ref, w_main_ref, w_qvt_ref, pool_w_ref, pool_scale_ref, bias_ref,
                  w_out_ref, g_out_ref, out_ref,
                  h_buf, k_buf, ind_buf, vt_buf, qt_buf, gate_buf, pgate_buf, y_buf, pv_buf, s_buf, m_buf):
    t = pl.program_id(1)
    tile = x_ref.shape[1]
    half = tile // 2
    n_groups = tile // GROUP
    halo_blocks = HALO // LANES

    @pl.when(t == 0)
    def _():
        pv_buf[0:POOL_HALO, :] = jnp.zeros((POOL_HALO, POOL_WIDTH), _F32)
        k_buf[0:HALO, :] = jnp.zeros((HALO, ATTN_WIDTH), _BF16)
        for b in range(halo_blocks):
            vt_buf[b] = jnp.zeros((ATTN_WIDTH, LANES), _BF16)
        lane = lax.broadcasted_iota(jnp.int32, (HALO, LANES), 1)
        ind_buf[0:HALO, :] = jnp.where(lane == 0, 1.0, 0.0).astype(_BF16)
        ind_buf[HALO:, :] = jnp.zeros((tile, LANES), _BF16)

    h_buf[...] = _rms_norm(x_ref[0], g_in_ref[...]).astype(_BF16)

    def proj_pieces(hf):
        rows = slice(hf * half, (hf + 1) * half)

        def main(part):
            lo = part * POOL_WIDTH
            return jnp.dot(h_buf[rows, :], w_main_ref[:, lo:lo + POOL_WIDTH], preferred_element_type=_F32)

        def pool_v():
            pv_buf[POOL_HALO + hf * half:POOL_HALO + (hf + 1) * half, :] = main(0)

        def pool_g():
            pgate_buf[rows, :] = _silu(main(1))

        def keys():
            k_buf[HALO + hf * half:HALO + (hf + 1) * half, :] = main(2).astype(_BF16)

        def attn_g():
            gate_buf[rows, :] = _silu(main(3))

        def transposed(part):
            w = w_qvt_ref[part * ATTN_WIDTH:(part + 1) * ATTN_WIDTH, :]
            return lax.dot_general(w, h_buf[rows, :], _NT, preferred_element_type=_F32)

        def queries():
            qt = transposed(0)
            for gg in range(half // GROUP):
                qt_buf[hf * (half // GROUP) + gg] = (qt[:, gg * GROUP:(gg + 1) * GROUP] * Q_SCALE).astype(_BF16)

        def values():
            vt = transposed(1)
            for gg in range(half // GROUP):
                vt_buf[halo_blocks + hf * (half // GROUP) + gg] = vt[:, gg * GROUP:(gg + 1) * GROUP].astype(_BF16)

        def pool(g):
            def run():
                w = POOL_WINDOWS[g]
                lanes = slice(g * POOL_GROUP_DIM, (g + 1) * POOL_GROUP_DIM)
                base = hf * half
                win = pv_buf[base:base + POOL_HALO + half, lanes]
                shift = 1
                while shift < w:
                    win = win + pltpu.roll(win, shift, 0)
                    shift *= 2
                win = win[POOL_HALO:]
                cur = pv_buf[POOL_HALO + base:POOL_HALO + base + half, lanes]
                head = POOL_HALO
                pos = t * tile + base + lax.broadcasted_iota(jnp.int32, (head, 1), 0)
                count = jnp.minimum(pos + 1, w).astype(_F32)
                mean = jnp.concatenate([win[:head] / count, win[head:] * (1.0 / w)], axis=0)
                d = (mean - cur).astype(_BF16)
                yg = jnp.dot(d, pool_w_ref[g], preferred_element_type=_F32)
                y_buf[rows, lanes] = (yg * pool_scale_ref[:, lanes] * pgate_buf[rows, lanes]).astype(_BF16)
            return run

        return [keys, queries, values, attn_g, pool_v, pool_g] + [pool(g) for g in range(len(POOL_WINDOWS))]

    def out_pieces(hf):
        rows = slice(hf * half, (hf + 1) * half)

        def project(c):
            def run():
                cols = slice(c * OUT_COLS, (c + 1) * OUT_COLS)
                out_ref[0, rows, cols] = x_ref[0, rows, cols] + jnp.dot(
                    y_buf[rows, :], w_out_ref[:, cols], preferred_element_type=_F32)
            return run

        def norm():
            out_ref[0, rows, :] = _rms_norm(out_ref[0, rows, :], g_out_ref[...])

        return [project(c) for c in range(D_MODEL // OUT_COLS)] + [norm]

    bufs = (k_buf, ind_buf, vt_buf, qt_buf, gate_buf, bias_ref, y_buf, s_buf, m_buf)
    groups_per_half = half // GROUP
    items_per_half = groups_per_half * N_PAIRS

    for piece in proj_pieces(0):
        piece()
    second = proj_pieces(1)
    _attend(list(range(n_groups)),
            _spread(second[:N_ATTN_PIECES], items_per_half - LOOKAHEAD + 1) + [[]] * (LOOKAHEAD - 1)
            + _spread(second[N_ATTN_PIECES:] + out_pieces(0), items_per_half), *bufs)
    for piece in out_pieces(1):
        piece()

    k_buf[0:HALO, :] = k_buf[tile:tile + HALO, :]
    for b in range(halo_blocks):
        vt_buf[b] = vt_buf[n_groups + b]
    pv_buf[0:POOL_HALO, :] = pv_buf[tile:tile + POOL_HALO, :]
    ind_buf[0:HALO, :] = jnp.zeros((HALO, LANES), _BF16)


def _fused_block(x, g_in, w_main, w_qvt, pool_w, pool_scale, bias, w_out, g_out, *, tile):
    batch, seq, d = x.shape
    assert d == D_MODEL and seq % tile == 0 and tile % (2 * GROUP) == 0 and tile >= HALO
    const2 = lambda b, t: (0, 0)
    const3 = lambda b, t: (0, 0, 0)
    return pl.pallas_call(
        _block_kernel,
        grid=(batch, seq // tile),
        in_specs=[
            pl.BlockSpec((1, tile, D_MODEL), lambda b, t: (b, t, 0)),
            pl.BlockSpec((1, D_MODEL), const2),
            pl.BlockSpec(w_main.shape, const2),
            pl.BlockSpec(w_qvt.shape, const2),
            pl.BlockSpec(pool_w.shape, const3),
            pl.BlockSpec((1, POOL_WIDTH), const2),
            pl.BlockSpec(bias.shape, const3),
            pl.BlockSpec(w_out.shape, const2),
            pl.BlockSpec((1, D_MODEL), const2),
        ],
        out_specs=pl.BlockSpec((1, tile, D_MODEL), lambda b, t: (b, t, 0)),
        out_shape=jax.ShapeDtypeStruct(x.shape, x.dtype),
        scratch_shapes=[
            pltpu.VMEM((tile, D_MODEL), _BF16),
            pltpu.VMEM((HALO + tile, ATTN_WIDTH), _BF16),
            pltpu.VMEM((HALO + tile, LANES), _BF16),
            pltpu.VMEM(((HALO + tile) // LANES, ATTN_WIDTH, LANES), _BF16),
            pltpu.VMEM((tile // GROUP, ATTN_WIDTH, GROUP), _BF16),
            pltpu.VMEM((tile, ATTN_WIDTH), _F32),
            pltpu.VMEM((tile, POOL_WIDTH), _F32),
            pltpu.VMEM((tile, POOL_WIDTH + ATTN_WIDTH), _BF16),
            pltpu.VMEM((POOL_HALO + tile, POOL_WIDTH), _F32),
            pltpu.VMEM((LOOKAHEAD + 1, GROUP_BAND, SCORE_COLS), _F32),
            pltpu.VMEM((LOOKAHEAD + 1, 1, SCORE_COLS), _F32),
        ],
        compiler_params=pltpu.CompilerParams(
            dimension_semantics=("arbitrary", "arbitrary"),
            vmem_limit_bytes=VMEM_LIMIT_BYTES,
        ),
        name="hybrid_block",
    )(x, g_in, w_main, w_qvt, pool_w, pool_scale, bias, w_out, g_out)


@jax.jit
def kernel(x, norm_gain, w_in, pool_w, pool_scale, rel_bias, w_out, final_norm_gain):
    w = w_in[0]
    pool_v, pool_g, wq, wk, wv, attn_g = (w[:, i * POOL_WIDTH:(i + 1) * POOL_WIDTH] for i in range(6))
    w_main = jnp.concatenate([pool_v, pool_g, wk, attn_g], axis=1).astype(_BF16)
    w_qvt = jnp.concatenate([wq, wv], axis=1).T.astype(_BF16)
    return _fused_block(
        x,
        norm_gain[0][None, :],
        w_main,
        w_qvt,
        pool_w[0].astype(_BF16),
        pool_scale[0][None, :],
        _expand_bias(rel_bias[0]),
        w_out[0].astype(_BF16),
        final_norm_gain[None, :],
        tile=SEQ_TILE,
    )
```

```python
import math

import jax
import jax.numpy as jnp
from jax import lax
from jax.experimental import pallas as pl
from jax.experimental.pallas import tpu as pltpu

D_MODEL = 1024
CHUNK = 64
POOL_WIDTH = 512
ATTN_WIDTH = 512
POOL_WINDOWS = (2, 4, 8, 16)
POOL_GROUP_DIM = 128
HEAD_DIM = 64
N_HEADS = 8
LEFT_CHUNKS = 8
HALO = LEFT_CHUNKS * CHUNK
MAX_REL = 64
N_REL = 2 * MAX_REL + 1
EPS = 1e-6
MASK_VALUE = -1e30
LOG2_E = math.log2(math.e)
Q_SCALE = LOG2_E / math.sqrt(HEAD_DIM)

LANES = 128
HEADS_PER_TILE = LANES // HEAD_DIM
N_PAIRS = N_HEADS // HEADS_PER_TILE
GROUP = LANES
GROUP_BAND = HALO + GROUP
SCORE_COLS = HEADS_PER_TILE * GROUP
BIAS_ROWS = GROUP + MAX_REL
SEQ_TILE = 1024
POOL_HALO = 16
OUT_COLS = 256
N_ATTN_PIECES = 4
LOOKAHEAD = 3
VMEM_LIMIT_BYTES = 60 * 1024 * 1024

_F32 = jnp.float32
_BF16 = jnp.bfloat16
_NT = (((1,), (1,)), ((), ()))


def _rms_norm(x, gain):
    return x * lax.rsqrt(jnp.mean(x * x, axis=-1, keepdims=True) + EPS) * gain


def _silu(x):
    return x / (1.0 + jnp.exp(-x))


def _bias_kernel(rel_lo_ref, rel_hi_ref, rel_0_ref, out_ref):
    j = lax.broadcasted_iota(jnp.int32, (BIAS_ROWS, GROUP), 0) + (GROUP_BAND - BIAS_ROWS)
    i = lax.broadcasted_iota(jnp.int32, (BIAS_ROWS, GROUP), 1)
    idx = jnp.clip(j - HALO - i, -MAX_REL, MAX_REL) + MAX_REL
    low = jnp.minimum(idx, LANES - 1)
    for head in range(N_HEADS):
        p, hh = divmod(head, HEADS_PER_TILE)
        row = jnp.broadcast_to(rel_lo_ref[head:head + 1, :], (BIAS_ROWS, LANES))
        table = jnp.take_along_axis(row, low, axis=1)
        table = jnp.where(idx == LANES, rel_hi_ref[head:head + 1, :], table)
        out_ref[p, :, hh * GROUP:(hh + 1) * GROUP] = (table - rel_0_ref[head:head + 1, :]) * LOG2_E


def _expand_bias(rel_bias):
    assert N_REL == LANES + 1
    rel_lo = rel_bias[:, :LANES]
    rel_hi = jnp.broadcast_to(rel_bias[:, LANES:], (N_HEADS, LANES))
    rel_0 = jnp.broadcast_to(rel_bias[:, :1], (N_HEADS, LANES))
    return pl.pallas_call(
        _bias_kernel,
        out_shape=jax.ShapeDtypeStruct((N_PAIRS, BIAS_ROWS, SCORE_COLS), _F32),
        name="rel_bias_expand",
    )(rel_lo, rel_hi, rel_0)


def _attend(groups, fillers, k_buf, ind_buf, vt_buf, qt_buf, gate_buf, bias_ref, y_buf, s_buf, m_buf):
    row = lax.broadcasted_iota(jnp.int32, (LANES, GROUP), 0)
    first_rows = row < HEAD_DIM
    lane = lax.broadcasted_iota(jnp.int32, (CHUNK, SCORE_COLS), 1)
    second_chunk = (lane & (GROUP - 1)) >= CHUNK
    pen_row = lax.broadcasted_iota(jnp.int32, (LANES, SCORE_COLS), 0)
    pen = jnp.where(pen_row == 0, MASK_VALUE, 0.0).astype(_BF16)
    items = [(g, p) for g in groups for p in range(N_PAIRS)]
    assert len(fillers) == len(items)

    def scores(i):
        g, p = items[i]
        lanes = slice(p * LANES, (p + 1) * LANES)
        band = slice(g * GROUP, g * GROUP + GROUP_BAND)
        lhs = jnp.concatenate([k_buf[band, lanes], ind_buf[band, :]], axis=1)
        qt = qt_buf[g, lanes, :]
        zero = jnp.zeros_like(qt)
        rhs = jnp.concatenate([jnp.where(first_rows, qt, zero), jnp.where(first_rows, zero, qt)], axis=1)
        rhs = jnp.concatenate([rhs, pen], axis=0)
        mid = GROUP_BAND // 2
        lo = GROUP_BAND - BIAS_ROWS
        bias = bias_ref[p]
        s0 = jnp.dot(lhs[:mid], rhs, preferred_element_type=_F32)
        s1 = jnp.dot(lhs[mid:], rhs, preferred_element_type=_F32)
        s0 = jnp.concatenate([jnp.where(second_chunk, MASK_VALUE, s0[:CHUNK]), s0[CHUNK:]], axis=0)
        s1 = jnp.concatenate([
            s1[:lo - mid],
            s1[lo - mid:-CHUNK] + bias[:BIAS_ROWS - CHUNK],
            jnp.where(second_chunk, s1[-CHUNK:] + bias[BIAS_ROWS - CHUNK:], MASK_VALUE),
        ], axis=0)
        slot = i % (LOOKAHEAD + 1)
        s_buf[slot, 0:mid, :] = s0
        s_buf[slot, mid:, :] = s1
        m_buf[slot] = jnp.maximum(jnp.max(s0, axis=0, keepdims=True), jnp.max(s1, axis=0, keepdims=True))

    def finish(i):
        g, p = items[i]
        lanes = slice(p * LANES, (p + 1) * LANES)
        rows = slice(g * GROUP, (g + 1) * GROUP)
        slot = i % (LOOKAHEAD + 1)
        e = jnp.exp2(s_buf[slot] - m_buf[slot])
        l = jnp.sum(e, axis=0, keepdims=True)
        vt = jnp.concatenate([vt_buf[g + b, lanes, :] for b in range(GROUP_BAND // LANES)], axis=1)
        eb = e.astype(_BF16)
        inv_l = 1.0 / l
        o = jnp.dot(vt, eb, preferred_element_type=_F32)
        ot = jnp.concatenate([o[:HEAD_DIM, :GROUP] * inv_l[:, :GROUP],
                              o[HEAD_DIM:, GROUP:] * inv_l[:, GROUP:]], axis=0)
        gated = ot.T * gate_buf[rows, lanes]
        y_buf[rows, POOL_WIDTH + p * LANES:POOL_WIDTH + (p + 1) * LANES] = gated.astype(_BF16)

    for i in range(min(LOOKAHEAD, len(items))):
        scores(i)
    for i in range(len(items)):
        for filler in fillers[i]:
            filler()
        if i + LOOKAHEAD < len(items):
            scores(i + LOOKAHEAD)
        finish(i)


def _spread(pieces, n):
    return [pieces[(i * len(pieces)) // n:((i + 1) * len(pieces)) // n] for i in range(n)]


def _block_kernel(x_ref, g_in_ref, w_main_ref, w_qvt_ref, pool_w_ref, pool_scale_ref, bias_ref,
                  w_out_ref, g_out_ref, out_ref,
                  h_buf, k_buf, ind_buf, vt_buf, qt_buf, gate_buf, pgate_buf, y_buf, pv_buf, s_buf, m_buf):
    t = pl.program_id(1)
    tile = x_ref.shape[1]
    half = tile // 2
    n_groups = tile // GROUP
    halo_blocks = HALO // LANES

    @pl.when(t == 0)
    def _():
        pv_buf[0:POOL_HALO, :] = jnp.zeros((POOL_HALO, POOL_WIDTH), _F32)
        k_buf[0:HALO, :] = jnp.zeros((HALO, ATTN_WIDTH), _BF16)
        for b in range(halo_blocks):
            vt_buf[b] = jnp.zeros((ATTN_WIDTH, LANES), _BF16)
        lane = lax.broadcasted_iota(jnp.int32, (HALO, LANES), 1)
        ind_buf[0:HALO, :] = jnp.where(lane == 0, 1.0, 0.0).astype(_BF16)
        ind_buf[HALO:, :] = jnp.zeros((tile, LANES), _BF16)

    h_buf[...] = _rms_norm(x_ref[0], g_in_ref[...]).astype(_BF16)

    def proj_pieces(hf):
        rows = slice(hf * half, (hf + 1) * half)

        def main(part):
            lo = part * POOL_WIDTH
            return jnp.dot(h_buf[rows, :], w_main_ref[:, lo:lo + POOL_WIDTH], preferred_element_type=_F32)

        def pool_v():
            pv_buf[POOL_HALO + hf * half:POOL_HALO + (hf + 1) * half, :] = main(0)

        def pool_g():
            pgate_buf[rows, :] = _silu(main(1))

        def keys():
            k_buf[HALO + hf * half:HALO + (hf + 1) * half, :] = main(2).astype(_BF16)

        def attn_g():
            gate_buf[rows, :] = _silu(main(3))

        def transposed(part):
            w = w_qvt_ref[part * ATTN_WIDTH:(part + 1) * ATTN_WIDTH, :]
            return lax.dot_general(w, h_buf[rows, :], _NT, preferred_element_type=_F32)

        def queries():
            qt = transposed(0)
            for gg in range(half // GROUP):
                qt_buf[hf * (half // GROUP) + gg] = (qt[:, gg * GROUP:(gg + 1) * GROUP] * Q_SCALE).astype(_BF16)

        def values():
            vt = transposed(1)
            for gg in range(half // GROUP):
                vt_buf[halo_blocks + hf * (half // GROUP) + gg] = vt[:, gg * GROUP:(gg + 1) * GROUP].astype(_BF16)

        def pool(g):
            def run():
                w = POOL_WINDOWS[g]
                lanes = slice(g * POOL_GROUP_DIM, (g + 1) * POOL_GROUP_DIM)
                base = hf * half
                win = pv_buf[base:base + POOL_HALO + half, lanes]
                shift = 1
                while shift < w:
                    win = win + pltpu.roll(win, shift, 0)
                    shift *= 2
                win = win[POOL_HALO:]
                cur = pv_buf[POOL_HALO + base:POOL_HALO + base + half, lanes]
                head = POOL_HALO
                pos = t * tile + base + lax.broadcasted_iota(jnp.int32, (head, 1), 0)
                count = jnp.minimum(pos + 1, w).astype(_F32)
                mean = jnp.concatenate([win[:head] / count, win[head:] * (1.0 / w)], axis=0)
                d = (mean - cur).astype(_BF16)
                yg = jnp.dot(d, pool_w_ref[g], preferred_element_type=_F32)
                y_buf[rows, lanes] = (yg * pool_scale_ref[:, lanes] * pgate_buf[rows, lanes]).astype(_BF16)
            return run

        return [keys, queries, values, attn_g, pool_v, pool_g] + [pool(g) for g in range(len(POOL_WINDOWS))]

    def out_pieces(hf):
        rows = slice(hf * half, (hf + 1) * half)

        def project(c):
            def run():
                cols = slice(c * OUT_COLS, (c + 1) * OUT_COLS)
                out_ref[0, rows, cols] = x_ref[0, rows, cols] + jnp.dot(
                    y_buf[rows, :], w_out_ref[:, cols], preferred_element_type=_F32)
            return run

        def norm():
            out_ref[0, rows, :] = _rms_norm(out_ref[0, rows, :], g_out_ref[...])

        return [project(c) for c in range(D_MODEL // OUT_COLS)] + [norm]

    bufs = (k_buf, ind_buf, vt_buf, qt_buf, gate_buf, bias_ref, y_buf, s_buf, m_buf)
    groups_per_half = half // GROUP
    items_per_half = groups_per_half * N_PAIRS

    for piece in proj_pieces(0):
        piece()
    second = proj_pieces(1)
    _attend(list(range(n_groups)),
            _spread(second[:N_ATTN_PIECES], items_per_half - LOOKAHEAD + 1) + [[]] * (LOOKAHEAD - 1)
            + _spread(second[N_ATTN_PIECES:] + out_pieces(0), items_per_half), *bufs)
    for piece in out_pieces(1):
        piece()

    k_buf[0:HALO, :] = k_buf[tile:tile + HALO, :]
    for b in range(halo_blocks):
        vt_buf[b] = vt_buf[n_groups + b]
    pv_buf[0:POOL_HALO, :] = pv_buf[tile:tile + POOL_HALO, :]
    ind_buf[0:HALO, :] = jnp.zeros((HALO, LANES), _BF16)


def _fused_block(x, g_in, w_main, w_qvt, pool_w, pool_scale, bias, w_out, g_out, *, tile):
    batch, seq, d = x.shape
    assert d == D_MODEL and seq % tile == 0 and tile % (2 * GROUP) == 0 and tile >= HALO

    def resident(shape):
        return pl.BlockSpec(shape, lambda b, t: (0,) * len(shape), pipeline_mode=pl.Buffered(1))

    return pl.pallas_call(
        _block_kernel,
        grid=(batch, seq // tile),
        in_specs=[
            pl.BlockSpec((1, tile, D_MODEL), lambda b, t: (b, t, 0)),
            resident((1, D_MODEL)),
            resident(w_main.shape),
            resident(w_qvt.shape),
            resident(pool_w.shape),
            resident((1, POOL_WIDTH)),
            resident(bias.shape),
            resident(w_out.shape),
            resident((1, D_MODEL)),
        ],
        out_specs=pl.BlockSpec((1, tile, D_MODEL), lambda b, t: (b, t, 0)),
        out_shape=jax.ShapeDtypeStruct(x.shape, x.dtype),
        scratch_shapes=[
            pltpu.VMEM((tile, D_MODEL), _BF16),
            pltpu.VMEM((HALO + tile, ATTN_WIDTH), _BF16),
            pltpu.VMEM((HALO + tile, LANES), _BF16),
            pltpu.VMEM(((HALO + tile) // LANES, ATTN_WIDTH, LANES), _BF16),
            pltpu.VMEM((tile // GROUP, ATTN_WIDTH, GROUP), _BF16),
            pltpu.VMEM((tile, ATTN_WIDTH), _F32),
            pltpu.VMEM((tile, POOL_WIDTH), _F32),
            pltpu.VMEM((tile, POOL_WIDTH + ATTN_WIDTH), _BF16),
            pltpu.VMEM((POOL_HALO + tile, POOL_WIDTH), _F32),
            pltpu.VMEM((LOOKAHEAD + 1, GROUP_BAND, SCORE_COLS), _F32),
            pltpu.VMEM((LOOKAHEAD + 1, 1, SCORE_COLS), _F32),
        ],
        compiler_params=pltpu.CompilerParams(
            dimension_semantics=("arbitrary", "arbitrary"),
            vmem_limit_bytes=VMEM_LIMIT_BYTES,
        ),
        name="hybrid_block",
    )(x, g_in, w_main, w_qvt, pool_w, pool_scale, bias, w_out, g_out)


@jax.jit
def kernel(x, norm_gain, w_in, pool_w, pool_scale, rel_bias, w_out, final_norm_gain):
    w = w_in[0]
    pool_v, pool_g, wq, wk, wv, attn_g = (w[:, i * POOL_WIDTH:(i + 1) * POOL_WIDTH] for i in range(6))
    w_main = jnp.concatenate([pool_v, pool_g, wk, attn_g], axis=1).astype(_BF16)
    w_qvt = jnp.concatenate([wq, wv], axis=1).T.astype(_BF16)
    return _fused_block(
        x,
        norm_gain[0][None, :],
        w_main,
        w_qvt,
        pool_w[0].astype(_BF16),
        pool_scale[0][None, :],
        _expand_bias(rel_bias[0]),
        w_out[0].astype(_BF16),
        final_norm_gain[None, :],
        tile=SEQ_TILE,
    )
```

```python
import math

import jax
import jax.numpy as jnp
from jax import lax
from jax.experimental import pallas as pl
from jax.experimental.pallas import tpu as pltpu

D_MODEL = 1024
CHUNK = 64
POOL_WIDTH = 512
ATTN_WIDTH = 512
POOL_WINDOWS = (2, 4, 8, 16)
POOL_GROUP_DIM = 128
HEAD_DIM = 64
N_HEADS = 8
LEFT_CHUNKS = 8
HALO = LEFT_CHUNKS * CHUNK
MAX_REL = 64
N_REL = 2 * MAX_REL + 1
EPS = 1e-6
MASK_VALUE = -1e30
LOG2_E = math.log2(math.e)
Q_SCALE = LOG2_E / math.sqrt(HEAD_DIM)

LANES = 128
HEADS_PER_TILE = LANES // HEAD_DIM
N_PAIRS = N_HEADS // HEADS_PER_TILE
GROUP = LANES
GROUP_BAND = HALO + GROUP
SCORE_COLS = HEADS_PER_TILE * GROUP
BIAS_ROWS = GROUP + MAX_REL
SEQ_TILE = 1024
POOL_HALO = 16
OUT_COLS = 256
N_ATTN_PIECES = 4
LOOKAHEAD = 3
VMEM_LIMIT_BYTES = 60 * 1024 * 1024

_F32 = jnp.float32
_BF16 = jnp.bfloat16
_NT = (((1,), (1,)), ((), ()))


def _rms_norm(x, gain):
    return x * lax.rsqrt(jnp.mean(x * x, axis=-1, keepdims=True) + EPS) * gain


def _silu(x):
    return x / (1.0 + jnp.exp(-x))


PREP_ROWS = 128
_POOL_V, _POOL_G, _Q, _K, _V, _ATTN_G = range(6)


def _bias_table(rel_ref, out_ref):
    assert N_REL == LANES + 1
    j = lax.broadcasted_iota(jnp.int32, (BIAS_ROWS, GROUP), 0) + (GROUP_BAND - BIAS_ROWS)
    i = lax.broadcasted_iota(jnp.int32, (BIAS_ROWS, GROUP), 1)
    idx = jnp.clip(j - HALO - i, -MAX_REL, MAX_REL) + MAX_REL
    low = jnp.minimum(idx, LANES - 1)
    for head in range(N_HEADS):
        p, hh = divmod(head, HEADS_PER_TILE)
        row = jnp.broadcast_to(rel_ref[head:head + 1, 0:LANES], (BIAS_ROWS, LANES))
        last = jnp.broadcast_to(rel_ref[head:head + 1, LANES:LANES + 1], (BIAS_ROWS, LANES))
        first = jnp.broadcast_to(rel_ref[head:head + 1, 0:1], (BIAS_ROWS, LANES))
        table = jnp.where(idx == LANES, last, jnp.take_along_axis(row, low, axis=1))
        out_ref[p, :, hh * GROUP:(hh + 1) * GROUP] = (table - first) * LOG2_E


def _prep_kernel(w_in_ref, w_out_ref, pool_w_ref, rel_ref,
                 w_main_ref, w_qvt_ref, w_out_bf_ref, pool_w_bf_ref, bias_ref):
    def cols(part):
        return w_in_ref[:, part * POOL_WIDTH:(part + 1) * POOL_WIDTH]

    w_main_ref[...] = jnp.concatenate([cols(_POOL_V), cols(_POOL_G), cols(_K), cols(_ATTN_G)], axis=1).astype(_BF16)
    w_qvt_ref[...] = jnp.concatenate([cols(_Q), cols(_V)], axis=1).T.astype(_BF16)
    w_out_bf_ref[...] = w_out_ref[...].astype(_BF16)

    @pl.when(pl.program_id(0) == 0)
    def _():
        pool_w_bf_ref[...] = pool_w_ref[...].astype(_BF16)
        _bias_table(rel_ref, bias_ref)


def _prepare_params(w_in, w_out, pool_w, rel_bias):
    n_main = 4 * POOL_WIDTH
    n_qv = 2 * ATTN_WIDTH

    def whole(a):
        return pl.BlockSpec(a.shape, lambda i: (0,) * a.ndim)

    bias_shape = (N_PAIRS, BIAS_ROWS, SCORE_COLS)
    return pl.pallas_call(
        _prep_kernel,
        grid=(D_MODEL // PREP_ROWS,),
        in_specs=[
            pl.BlockSpec((PREP_ROWS, w_in.shape[1]), lambda i: (i, 0)),
            pl.BlockSpec((PREP_ROWS, D_MODEL), lambda i: (i, 0)),
            whole(pool_w),
            whole(rel_bias),
        ],
        out_specs=[
            pl.BlockSpec((PREP_ROWS, n_main), lambda i: (i, 0)),
            pl.BlockSpec((n_qv, PREP_ROWS), lambda i: (0, i)),
            pl.BlockSpec((PREP_ROWS, D_MODEL), lambda i: (i, 0)),
            pl.BlockSpec(pool_w.shape, lambda i: (0, 0, 0)),
            pl.BlockSpec(bias_shape, lambda i: (0, 0, 0)),
        ],
        out_shape=[
            jax.ShapeDtypeStruct((D_MODEL, n_main), _BF16),
            jax.ShapeDtypeStruct((n_qv, D_MODEL), _BF16),
            jax.ShapeDtypeStruct((POOL_WIDTH + ATTN_WIDTH, D_MODEL), _BF16),
            jax.ShapeDtypeStruct(pool_w.shape, _BF16),
            jax.ShapeDtypeStruct(bias_shape, _F32),
        ],
        compiler_params=pltpu.CompilerParams(dimension_semantics=("arbitrary",)),
        name="prepare_params",
    )(w_in, w_out, pool_w, rel_bias)


def _attend(groups, fillers, k_buf, ind_buf, vt_buf, qt_buf, gate_buf, bias_ref, y_buf, s_buf, m_buf):
    row = lax.broadcasted_iota(jnp.int32, (LANES, GROUP), 0)
    first_rows = row < HEAD_DIM
    lane = lax.broadcasted_iota(jnp.int32, (CHUNK, SCORE_COLS), 1)
    second_chunk = (lane & (GROUP - 1)) >= CHUNK
    pen_row = lax.broadcasted_iota(jnp.int32, (LANES, SCORE_COLS), 0)
    pen = jnp.where(pen_row == 0, MASK_VALUE, 0.0).astype(_BF16)
    items = [(g, p) for g in groups for p in range(N_PAIRS)]
    assert len(fillers) == len(items)

    def scores(i):
        g, p = items[i]
        lanes = slice(p * LANES, (p + 1) * LANES)
        band = slice(g * GROUP, g * GROUP + GROUP_BAND)
        lhs = jnp.concatenate([k_buf[band, lanes], ind_buf[band, :]], axis=1)
        qt = qt_buf[g, lanes, :]
        zero = jnp.zeros_like(qt)
        rhs = jnp.concatenate([jnp.where(first_rows, qt, zero), jnp.where(first_rows, zero, qt)], axis=1)
        rhs = jnp.concatenate([rhs, pen], axis=0)
        mid = GROUP_BAND // 2
        lo = GROUP_BAND - BIAS_ROWS
        bias = bias_ref[p]
        s0 = jnp.dot(lhs[:mid], rhs, preferred_element_type=_F32)
        s1 = jnp.dot(lhs[mid:], rhs, preferred_element_type=_F32)
        s0 = jnp.concatenate([jnp.where(second_chunk, MASK_VALUE, s0[:CHUNK]), s0[CHUNK:]], axis=0)
        s1 = jnp.concatenate([
            s1[:lo - mid],
            s1[lo - mid:-CHUNK] + bias[:BIAS_ROWS - CHUNK],
            jnp.where(second_chunk, s1[-CHUNK:] + bias[BIAS_ROWS - CHUNK:], MASK_VALUE),
        ], axis=0)
        slot = i % (LOOKAHEAD + 1)
        s_buf[slot, 0:mid, :] = s0
        s_buf[slot, mid:, :] = s1
        m_buf[slot] = jnp.maximum(jnp.max(s0, axis=0, keepdims=True), jnp.max(s1, axis=0, keepdims=True))

    def finish(i):
        g, p = items[i]
        lanes = slice(p * LANES, (p + 1) * LANES)
        rows = slice(g * GROUP, (g + 1) * GROUP)
        slot = i % (LOOKAHEAD + 1)
        e = jnp.exp2(s_buf[slot] - m_buf[slot])
        l = jnp.sum(e, axis=0, keepdims=True)
        vt = jnp.concatenate([vt_buf[g + b, lanes, :] for b in range(GROUP_BAND // LANES)], axis=1)
        o = jnp.dot(vt, e.astype(_BF16), preferred_element_type=_F32)
        inv_l = 1.0 / l
        ot = jnp.concatenate([o[:HEAD_DIM, :GROUP] * inv_l[:, :GROUP],
                              o[HEAD_DIM:, GROUP:] * inv_l[:, GROUP:]], axis=0)
        gated = ot.T * gate_buf[rows, lanes]
        y_buf[rows, POOL_WIDTH + p * LANES:POOL_WIDTH + (p + 1) * LANES] = gated.astype(_BF16)

    for i in range(min(LOOKAHEAD, len(items))):
        scores(i)
    for i in range(len(items)):
        for filler in fillers[i]:
            filler()
        if i + LOOKAHEAD < len(items):
            scores(i + LOOKAHEAD)
        finish(i)


def _spread(pieces, n):
    return [pieces[(i * len(pieces)) // n:((i + 1) * len(pieces)) // n] for i in range(n)]


def _block_kernel(x_ref, g_in_ref, w_main_ref, w_qvt_ref, pool_w_ref, pool_scale_ref, bias_ref,
                  w_out_ref, g_out_ref, out_ref,
                  h_buf, k_buf, ind_buf, vt_buf, qt_buf, gate_buf, pgate_buf, y_buf, pv_buf, s_buf, m_buf):
    t = pl.program_id(1)
    tile = x_ref.shape[1]
    half = tile // 2
    n_groups = tile // GROUP
    halo_blocks = HALO // LANES

    @pl.when(t == 0)
    def _():
        pv_buf[0:POOL_HALO, :] = jnp.zeros((POOL_HALO, POOL_WIDTH), _F32)
        k_buf[0:HALO, :] = jnp.zeros((HALO, ATTN_WIDTH), _BF16)
        for b in range(halo_blocks):
            vt_buf[b] = jnp.zeros((ATTN_WIDTH, LANES), _BF16)
        lane = lax.broadcasted_iota(jnp.int32, (HALO, LANES), 1)
        ind_buf[0:HALO, :] = jnp.where(lane == 0, 1.0, 0.0).astype(_BF16)
        ind_buf[HALO:, :] = jnp.zeros((tile, LANES), _BF16)

    h_buf[...] = _rms_norm(x_ref[0], g_in_ref[...]).astype(_BF16)

    def proj_pieces(hf):
        rows = slice(hf * half, (hf + 1) * half)

        def main(part):
            lo = part * POOL_WIDTH
            return jnp.dot(h_buf[rows, :], w_main_ref[:, lo:lo + POOL_WIDTH], preferred_element_type=_F32)

        def pool_v():
            pv_buf[POOL_HALO + hf * half:POOL_HALO + (hf + 1) * half, :] = main(0)

        def pool_g():
            pgate_buf[rows, :] = _silu(main(1))

        def keys():
            k_buf[HALO + hf * half:HALO + (hf + 1) * half, :] = main(2).astype(_BF16)

        def attn_g():
            gate_buf[rows, :] = _silu(main(3))

        def transposed(part):
            w = w_qvt_ref[part * ATTN_WIDTH:(part + 1) * ATTN_WIDTH, :]
            return lax.dot_general(w, h_buf[rows, :], _NT, preferred_element_type=_F32)

        def queries():
            qt = transposed(0)
            for gg in range(half // GROUP):
                qt_buf[hf * (half // GROUP) + gg] = (qt[:, gg * GROUP:(gg + 1) * GROUP] * Q_SCALE).astype(_BF16)

        def values():
            vt = transposed(1)
            for gg in range(half // GROUP):
                vt_buf[halo_blocks + hf * (half // GROUP) + gg] = vt[:, gg * GROUP:(gg + 1) * GROUP].astype(_BF16)

        def pool(g):
            def run():
                w = POOL_WINDOWS[g]
                lanes = slice(g * POOL_GROUP_DIM, (g + 1) * POOL_GROUP_DIM)
                base = hf * half
                win = pv_buf[base:base + POOL_HALO + half, lanes]
                shift = 1
                while shift < w:
                    win = win + pltpu.roll(win, shift, 0)
                    shift *= 2
                win = win[POOL_HALO:]
                cur = pv_buf[POOL_HALO + base:POOL_HALO + base + half, lanes]
                head = POOL_HALO
                pos = t * tile + base + lax.broadcasted_iota(jnp.int32, (head, 1), 0)
                count = jnp.minimum(pos + 1, w).astype(_F32)
                mean = jnp.concatenate([win[:head] / count, win[head:] * (1.0 / w)], axis=0)
                d = (mean - cur).astype(_BF16)
                yg = jnp.dot(d, pool_w_ref[g], preferred_element_type=_F32)
                y_buf[rows, lanes] = (yg * pool_scale_ref[:, lanes] * pgate_buf[rows, lanes]).astype(_BF16)
            return run

        return [keys, queries, values, attn_g, pool_v, pool_g] + [pool(g) for g in range(len(POOL_WINDOWS))]

    def out_pieces(hf):
        rows = slice(hf * half, (hf + 1) * half)

        def project(c):
            def run():
                cols = slice(c * OUT_COLS, (c + 1) * OUT_COLS)
                out_ref[0, rows, cols] = x_ref[0, rows, cols] + jnp.dot(
                    y_buf[rows, :], w_out_ref[:, cols], preferred_element_type=_F32)
            return run

        def norm():
            out_ref[0, rows, :] = _rms_norm(out_ref[0, rows, :], g_out_ref[...])

        return [project(c) for c in range(D_MODEL // OUT_COLS)] + [norm]

    bufs = (k_buf, ind_buf, vt_buf, qt_buf, gate_buf, bias_ref, y_buf, s_buf, m_buf)
    groups_per_half = half // GROUP
    items_per_half = groups_per_half * N_PAIRS

    for piece in proj_pieces(0):
        piece()
    second = proj_pieces(1)
    _attend(list(range(n_groups)),
            _spread(second[:N_ATTN_PIECES], items_per_half - LOOKAHEAD + 1) + [[]] * (LOOKAHEAD - 1)
            + _spread(second[N_ATTN_PIECES:] + out_pieces(0), items_per_half), *bufs)
    for piece in out_pieces(1):
        piece()

    k_buf[0:HALO, :] = k_buf[tile:tile + HALO, :]
    for b in range(halo_blocks):
        vt_buf[b] = vt_buf[n_groups + b]
    pv_buf[0:POOL_HALO, :] = pv_buf[tile:tile + POOL_HALO, :]
    ind_buf[0:HALO, :] = jnp.zeros((HALO, LANES), _BF16)


def _fused_block(x, g_in, w_main, w_qvt, pool_w, pool_scale, bias, w_out, g_out, *, tile):
    batch, seq, d = x.shape
    assert d == D_MODEL and seq % tile == 0 and tile % (2 * GROUP) == 0 and tile >= HALO

    def resident(shape):
        return pl.BlockSpec(shape, lambda b, t: (0,) * len(shape), pipeline_mode=pl.Buffered(1))

    return pl.pallas_call(
        _block_kernel,
        grid=(batch, seq // tile),
        in_specs=[
            pl.BlockSpec((1, tile, D_MODEL), lambda b, t: (b, t, 0)),
            resident((1, D_MODEL)),
            resident(w_main.shape),
            resident(w_qvt.shape),
            resident(pool_w.shape),
            resident((1, POOL_WIDTH)),
            resident(bias.shape),
            resident(w_out.shape),
            resident((1, D_MODEL)),
        ],
        out_specs=pl.BlockSpec((1, tile, D_MODEL), lambda b, t: (b, t, 0)),
        out_shape=jax.ShapeDtypeStruct(x.shape, x.dtype),
        scratch_shapes=[
            pltpu.VMEM((tile, D_MODEL), _BF16),
            pltpu.VMEM((HALO + tile, ATTN_WIDTH), _BF16),
            pltpu.VMEM((HALO + tile, LANES), _BF16),
            pltpu.VMEM(((HALO + tile) // LANES, ATTN_WIDTH, LANES), _BF16),
            pltpu.VMEM((tile // GROUP, ATTN_WIDTH, GROUP), _BF16),
            pltpu.VMEM((tile, ATTN_WIDTH), _F32),
            pltpu.VMEM((tile, POOL_WIDTH), _F32),
            pltpu.VMEM((tile, POOL_WIDTH + ATTN_WIDTH), _BF16),
            pltpu.VMEM((POOL_HALO + tile, POOL_WIDTH), _F32),
            pltpu.VMEM((LOOKAHEAD + 1, GROUP_BAND, SCORE_COLS), _F32),
            pltpu.VMEM((LOOKAHEAD + 1, 1, SCORE_COLS), _F32),
        ],
        compiler_params=pltpu.CompilerParams(
            dimension_semantics=("arbitrary", "arbitrary"),
            vmem_limit_bytes=VMEM_LIMIT_BYTES,
        ),
        name="hybrid_block",
    )(x, g_in, w_main, w_qvt, pool_w, pool_scale, bias, w_out, g_out)


@jax.jit
def kernel(x, norm_gain, w_in, pool_w, pool_scale, rel_bias, w_out, final_norm_gain):
    w_main, w_qvt, w_out_bf, pool_w_bf, bias = _prepare_params(w_in[0], w_out[0], pool_w[0], rel_bias[0])
    return _fused_block(
        x,
        norm_gain[0][None, :],
        w_main,
        w_qvt,
        pool_w_bf,
        pool_scale[0][None, :],
        bias,
        w_out_bf,
        final_norm_gain[None, :],
        tile=SEQ_TILE,
    )
```

```python
import math

import jax
import jax.numpy as jnp
from jax import lax
from jax.experimental import pallas as pl
from jax.experimental.pallas import tpu as pltpu

D_MODEL = 1024
CHUNK = 64
POOL_WIDTH = 512
ATTN_WIDTH = 512
POOL_WINDOWS = (2, 4, 8, 16)
POOL_GROUP_DIM = 128
HEAD_DIM = 64
N_HEADS = 8
LEFT_CHUNKS = 8
HALO = LEFT_CHUNKS * CHUNK
MAX_REL = 64
N_REL = 2 * MAX_REL + 1
EPS = 1e-6
MASK_VALUE = -1e30
LOG2_E = math.log2(math.e)
Q_SCALE = LOG2_E / math.sqrt(HEAD_DIM)

LANES = 128
HEADS_PER_TILE = LANES // HEAD_DIM
N_PAIRS = N_HEADS // HEADS_PER_TILE
GROUP = LANES
GROUP_BAND = HALO + GROUP
SCORE_COLS = HEADS_PER_TILE * GROUP
BIAS_ROWS = GROUP + MAX_REL
SEQ_TILE = 1024
POOL_HALO = 16
OUT_COLS = 256
OUT_ROWS = 256
N_ATTN_PIECES = 4
LOOKAHEAD = 3
VMEM_LIMIT_BYTES = 60 * 1024 * 1024

_F32 = jnp.float32
_BF16 = jnp.bfloat16
_NT = (((1,), (1,)), ((), ()))


def _rms_norm(x, gain):
    return x * lax.rsqrt(jnp.mean(x * x, axis=-1, keepdims=True) + EPS) * gain


def _silu(x):
    return x / (1.0 + jnp.exp(-x))


PREP_ROWS = 128
_POOL_V, _POOL_G, _Q, _K, _V, _ATTN_G = range(6)


def _bias_table(rel_ref, out_ref):
    assert N_REL == LANES + 1
    j = lax.broadcasted_iota(jnp.int32, (BIAS_ROWS, GROUP), 0) + (GROUP_BAND - BIAS_ROWS)
    i = lax.broadcasted_iota(jnp.int32, (BIAS_ROWS, GROUP), 1)
    idx = jnp.clip(j - HALO - i, -MAX_REL, MAX_REL) + MAX_REL
    low = jnp.minimum(idx, LANES - 1)
    for head in range(N_HEADS):
        p, hh = divmod(head, HEADS_PER_TILE)
        row = jnp.broadcast_to(rel_ref[head:head + 1, 0:LANES], (BIAS_ROWS, LANES))
        last = jnp.broadcast_to(rel_ref[head:head + 1, LANES:LANES + 1], (BIAS_ROWS, LANES))
        first = jnp.broadcast_to(rel_ref[head:head + 1, 0:1], (BIAS_ROWS, LANES))
        table = jnp.where(idx == LANES, last, jnp.take_along_axis(row, low, axis=1))
        out_ref[p, :, hh * GROUP:(hh + 1) * GROUP] = (table - first) * LOG2_E


def _prep_kernel(w_in_ref, w_out_ref, pool_w_ref, rel_ref,
                 w_main_ref, w_qvt_ref, w_out_bf_ref, pool_w_bf_ref, bias_ref):
    def cols(part):
        return w_in_ref[:, part * POOL_WIDTH:(part + 1) * POOL_WIDTH]

    w_main_ref[...] = jnp.concatenate([cols(_POOL_V), cols(_POOL_G), cols(_K), cols(_ATTN_G)], axis=1).astype(_BF16)
    w_qvt_ref[...] = jnp.concatenate([cols(_Q), cols(_V)], axis=1).T.astype(_BF16)
    w_out_bf_ref[...] = w_out_ref[...].astype(_BF16)

    @pl.when(pl.program_id(0) == 0)
    def _():
        pool_w_bf_ref[...] = pool_w_ref[...].astype(_BF16)
        _bias_table(rel_ref, bias_ref)


def _prepare_params(w_in, w_out, pool_w, rel_bias):
    n_main = 4 * POOL_WIDTH
    n_qv = 2 * ATTN_WIDTH

    def whole(a):
        return pl.BlockSpec(a.shape, lambda i: (0,) * a.ndim)

    bias_shape = (N_PAIRS, BIAS_ROWS, SCORE_COLS)
    return pl.pallas_call(
        _prep_kernel,
        grid=(D_MODEL // PREP_ROWS,),
        in_specs=[
            pl.BlockSpec((PREP_ROWS, w_in.shape[1]), lambda i: (i, 0)),
            pl.BlockSpec((PREP_ROWS, D_MODEL), lambda i: (i, 0)),
            whole(pool_w),
            whole(rel_bias),
        ],
        out_specs=[
            pl.BlockSpec((PREP_ROWS, n_main), lambda i: (i, 0)),
            pl.BlockSpec((n_qv, PREP_ROWS), lambda i: (0, i)),
            pl.BlockSpec((PREP_ROWS, D_MODEL), lambda i: (i, 0)),
            pl.BlockSpec(pool_w.shape, lambda i: (0, 0, 0)),
            pl.BlockSpec(bias_shape, lambda i: (0, 0, 0)),
        ],
        out_shape=[
            jax.ShapeDtypeStruct((D_MODEL, n_main), _BF16),
            jax.ShapeDtypeStruct((n_qv, D_MODEL), _BF16),
            jax.ShapeDtypeStruct((POOL_WIDTH + ATTN_WIDTH, D_MODEL), _BF16),
            jax.ShapeDtypeStruct(pool_w.shape, _BF16),
            jax.ShapeDtypeStruct(bias_shape, _F32),
        ],
        compiler_params=pltpu.CompilerParams(dimension_semantics=("arbitrary",)),
        name="prepare_params",
    )(w_in, w_out, pool_w, rel_bias)


def _attend(groups, fillers, k_buf, ind_buf, vt_buf, qt_buf, gate_buf, bias_ref, y_buf, s_buf, m_buf):
    row = lax.broadcasted_iota(jnp.int32, (LANES, GROUP), 0)
    first_rows = row < HEAD_DIM
    lane = lax.broadcasted_iota(jnp.int32, (CHUNK, SCORE_COLS), 1)
    second_chunk = (lane & (GROUP - 1)) >= CHUNK
    pen_row = lax.broadcasted_iota(jnp.int32, (LANES, SCORE_COLS), 0)
    pen = jnp.where(pen_row == 0, MASK_VALUE, 0.0).astype(_BF16)
    items = [(g, p) for g in groups for p in range(N_PAIRS)]
    assert len(fillers) == len(items)

    def scores(i):
        g, p = items[i]
        lanes = slice(p * LANES, (p + 1) * LANES)
        band = slice(g * GROUP, g * GROUP + GROUP_BAND)
        lhs = jnp.concatenate([k_buf[band, lanes], ind_buf[band, :]], axis=1)
        qt = qt_buf[g, lanes, :]
        zero = jnp.zeros_like(qt)
        rhs = jnp.concatenate([jnp.where(first_rows, qt, zero), jnp.where(first_rows, zero, qt)], axis=1)
        rhs = jnp.concatenate([rhs, pen], axis=0)
        mid = GROUP_BAND // 2
        lo = GROUP_BAND - BIAS_ROWS
        bias = bias_ref[p]
        s0 = jnp.dot(lhs[:mid], rhs, preferred_element_type=_F32)
        s1 = jnp.dot(lhs[mid:], rhs, preferred_element_type=_F32)
        s0 = jnp.concatenate([jnp.where(second_chunk, MASK_VALUE, s0[:CHUNK]), s0[CHUNK:]], axis=0)
        s1 = jnp.concatenate([
            s1[:lo - mid],
            s1[lo - mid:-CHUNK] + bias[:BIAS_ROWS - CHUNK],
            jnp.where(second_chunk, s1[-CHUNK:] + bias[BIAS_ROWS - CHUNK:], MASK_VALUE),
        ], axis=0)
        slot = i % (LOOKAHEAD + 1)
        s_buf[slot, 0:mid, :] = s0
        s_buf[slot, mid:, :] = s1
        m_buf[slot] = jnp.maximum(jnp.max(s0, axis=0, keepdims=True), jnp.max(s1, axis=0, keepdims=True))

    def finish(i):
        g, p = items[i]
        lanes = slice(p * LANES, (p + 1) * LANES)
        rows = slice(g * GROUP, (g + 1) * GROUP)
        slot = i % (LOOKAHEAD + 1)
        e = jnp.exp2(s_buf[slot] - m_buf[slot])
        l = jnp.sum(e, axis=0, keepdims=True)
        vt = jnp.concatenate([vt_buf[g + b, lanes, :] for b in range(GROUP_BAND // LANES)], axis=1)
        o = jnp.dot(vt, e.astype(_BF16), preferred_element_type=_F32)
        inv_l = 1.0 / l
        ot = jnp.concatenate([o[:HEAD_DIM, :GROUP] * inv_l[:, :GROUP],
                              o[HEAD_DIM:, GROUP:] * inv_l[:, GROUP:]], axis=0)
        gated = ot.T * gate_buf[rows, lanes]
        y_buf[rows, POOL_WIDTH + p * LANES:POOL_WIDTH + (p + 1) * LANES] = gated.astype(_BF16)

    for i in range(min(LOOKAHEAD, len(items))):
        scores(i)
    for i in range(len(items)):
        for filler in fillers[i]:
            filler()
        if i + LOOKAHEAD < len(items):
            scores(i + LOOKAHEAD)
        finish(i)


def _spread(pieces, n):
    return [pieces[(i * len(pieces)) // n:((i + 1) * len(pieces)) // n] for i in range(n)]


def _block_kernel(x_ref, g_in_ref, w_main_ref, w_qvt_ref, pool_w_ref, pool_scale_ref, bias_ref,
                  w_out_ref, g_out_ref, out_ref,
                  h_buf, k_buf, ind_buf, vt_buf, qt_buf, gate_buf, pgate_buf, y_buf, pv_buf, s_buf, m_buf):
    t = pl.program_id(1)
    tile = x_ref.shape[1]
    half = tile // 2
    n_groups = tile // GROUP
    halo_blocks = HALO // LANES

    @pl.when(t == 0)
    def _():
        pv_buf[0:POOL_HALO, :] = jnp.zeros((POOL_HALO, POOL_WIDTH), _F32)
        k_buf[0:HALO, :] = jnp.zeros((HALO, ATTN_WIDTH), _BF16)
        for b in range(halo_blocks):
            vt_buf[b] = jnp.zeros((ATTN_WIDTH, LANES), _BF16)
        lane = lax.broadcasted_iota(jnp.int32, (HALO, LANES), 1)
        ind_buf[0:HALO, :] = jnp.where(lane == 0, 1.0, 0.0).astype(_BF16)
        ind_buf[HALO:, :] = jnp.zeros((tile, LANES), _BF16)

    h_buf[...] = _rms_norm(x_ref[0], g_in_ref[...]).astype(_BF16)

    def proj_pieces(hf):
        rows = slice(hf * half, (hf + 1) * half)

        def main(part, rows=rows):
            lo = part * POOL_WIDTH
            return jnp.dot(h_buf[rows, :], w_main_ref[:, lo:lo + POOL_WIDTH], preferred_element_type=_F32)

        def pool_v():
            pv_buf[POOL_HALO + hf * half:POOL_HALO + (hf + 1) * half, :] = main(0)

        def pool_g():
            pgate_buf[rows, :] = _silu(main(1))

        def keys():
            for r0 in range(hf * half, (hf + 1) * half, OUT_ROWS):
                k_buf[HALO + r0:HALO + r0 + OUT_ROWS, :] = main(2, slice(r0, r0 + OUT_ROWS)).astype(_BF16)

        def attn_g():
            gate_buf[rows, :] = _silu(main(3))

        def transposed(part):
            w = w_qvt_ref[part * ATTN_WIDTH:(part + 1) * ATTN_WIDTH, :]
            return lax.dot_general(w, h_buf[rows, :], _NT, preferred_element_type=_F32)

        def queries():
            qt = transposed(0)
            for gg in range(half // GROUP):
                qt_buf[hf * (half // GROUP) + gg] = (qt[:, gg * GROUP:(gg + 1) * GROUP] * Q_SCALE).astype(_BF16)

        def values():
            vt = transposed(1)
            for gg in range(half // GROUP):
                vt_buf[halo_blocks + hf * (half // GROUP) + gg] = vt[:, gg * GROUP:(gg + 1) * GROUP].astype(_BF16)

        def pool(g):
            def run():
                w = POOL_WINDOWS[g]
                lanes = slice(g * POOL_GROUP_DIM, (g + 1) * POOL_GROUP_DIM)
                base = hf * half
                win = pv_buf[base:base + POOL_HALO + half, lanes]
                shift = 1
                while shift < w:
                    win = win + pltpu.roll(win, shift, 0)
                    shift *= 2
                win = win[POOL_HALO:]
                cur = pv_buf[POOL_HALO + base:POOL_HALO + base + half, lanes]
                head = POOL_HALO
                pos = t * tile + base + lax.broadcasted_iota(jnp.int32, (head, 1), 0)
                count = jnp.minimum(pos + 1, w).astype(_F32)
                mean = jnp.concatenate([win[:head] / count, win[head:] * (1.0 / w)], axis=0)
                d = (mean - cur).astype(_BF16)
                yg = jnp.dot(d, pool_w_ref[g], preferred_element_type=_F32)
                y_buf[rows, lanes] = (yg * pool_scale_ref[:, lanes] * pgate_buf[rows, lanes]).astype(_BF16)
            return run

        return [keys, queries, values, attn_g, pool_v, pool_g] + [pool(g) for g in range(len(POOL_WINDOWS))]

    def out_pieces(hf):
        def project(rows, c):
            def run():
                cols = slice(c * OUT_COLS, (c + 1) * OUT_COLS)
                out_ref[0, rows, cols] = x_ref[0, rows, cols] + jnp.dot(
                    y_buf[rows, :], w_out_ref[:, cols], preferred_element_type=_F32)
            return run

        def norm(rows):
            def run():
                out_ref[0, rows, :] = _rms_norm(out_ref[0, rows, :], g_out_ref[...])
            return run

        pieces = []
        for r0 in range(hf * half, (hf + 1) * half, OUT_ROWS):
            rows = slice(r0, r0 + OUT_ROWS)
            pieces += [project(rows, c) for c in range(D_MODEL // OUT_COLS)] + [norm(rows)]
        return pieces

    bufs = (k_buf, ind_buf, vt_buf, qt_buf, gate_buf, bias_ref, y_buf, s_buf, m_buf)
    groups_per_half = half // GROUP
    items_per_half = groups_per_half * N_PAIRS

    for piece in proj_pieces(0):
        piece()
    second = proj_pieces(1)
    _attend(list(range(n_groups)),
            _spread(second[:N_ATTN_PIECES], items_per_half - LOOKAHEAD + 1) + [[]] * (LOOKAHEAD - 1)
            + _spread(second[N_ATTN_PIECES:] + out_pieces(0), items_per_half), *bufs)
    for piece in out_pieces(1):
        piece()

    k_buf[0:HALO, :] = k_buf[tile:tile + HALO, :]
    for b in range(halo_blocks):
        vt_buf[b] = vt_buf[n_groups + b]
    pv_buf[0:POOL_HALO, :] = pv_buf[tile:tile + POOL_HALO, :]
    ind_buf[0:HALO, :] = jnp.zeros((HALO, LANES), _BF16)


def _fused_block(x, g_in, w_main, w_qvt, pool_w, pool_scale, bias, w_out, g_out, *, tile):
    batch, seq, d = x.shape
    assert d == D_MODEL and seq % tile == 0 and tile % (2 * GROUP) == 0 and tile >= HALO

    def resident(shape):
        return pl.BlockSpec(shape, lambda b, t: (0,) * len(shape), pipeline_mode=pl.Buffered(1))

    return pl.pallas_call(
        _block_kernel,
        grid=(batch, seq // tile),
        in_specs=[
            pl.BlockSpec((1, tile, D_MODEL), lambda b, t: (b, t, 0)),
            resident((1, D_MODEL)),
            resident(w_main.shape),
            resident(w_qvt.shape),
            resident(pool_w.shape),
            resident((1, POOL_WIDTH)),
            resident(bias.shape),
            resident(w_out.shape),
            resident((1, D_MODEL)),
        ],
        out_specs=pl.BlockSpec((1, tile, D_MODEL), lambda b, t: (b, t, 0)),
        out_shape=jax.ShapeDtypeStruct(x.shape, x.dtype),
        scratch_shapes=[
            pltpu.VMEM((tile, D_MODEL), _BF16),
            pltpu.VMEM((HALO + tile, ATTN_WIDTH), _BF16),
            pltpu.VMEM((HALO + tile, LANES), _BF16),
            pltpu.VMEM(((HALO + tile) // LANES, ATTN_WIDTH, LANES), _BF16),
            pltpu.VMEM((tile // GROUP, ATTN_WIDTH, GROUP), _BF16),
            pltpu.VMEM((tile, ATTN_WIDTH), _F32),
            pltpu.VMEM((tile, POOL_WIDTH), _F32),
            pltpu.VMEM((tile, POOL_WIDTH + ATTN_WIDTH), _BF16),
            pltpu.VMEM((POOL_HALO + tile, POOL_WIDTH), _F32),
            pltpu.VMEM((LOOKAHEAD + 1, GROUP_BAND, SCORE_COLS), _F32),
            pltpu.VMEM((LOOKAHEAD + 1, 1, SCORE_COLS), _F32),
        ],
        compiler_params=pltpu.CompilerParams(
            dimension_semantics=("arbitrary", "arbitrary"),
            vmem_limit_bytes=VMEM_LIMIT_BYTES,
        ),
        name="hybrid_block",
    )(x, g_in, w_main, w_qvt, pool_w, pool_scale, bias, w_out, g_out)


@jax.jit
def kernel(x, norm_gain, w_in, pool_w, pool_scale, rel_bias, w_out, final_norm_gain):
    w_main, w_qvt, w_out_bf, pool_w_bf, bias = _prepare_params(w_in[0], w_out[0], pool_w[0], rel_bias[0])
    return _fused_block(
        x,
        norm_gain[0][None, :],
        w_main,
        w_qvt,
        pool_w_bf,
        pool_scale[0][None, :],
        bias,
        w_out_bf,
        final_norm_gain[None, :],
        tile=SEQ_TILE,
    )
```

```python
import math

import jax
import jax.numpy as jnp
from jax import lax
from jax.experimental import pallas as pl
from jax.experimental.pallas import tpu as pltpu

D_MODEL = 1024
CHUNK = 64
POOL_WIDTH = 512
ATTN_WIDTH = 512
POOL_WINDOWS = (2, 4, 8, 16)
POOL_GROUP_DIM = 128
HEAD_DIM = 64
N_HEADS = 8
LEFT_CHUNKS = 8
HALO = LEFT_CHUNKS * CHUNK
MAX_REL = 64
N_REL = 2 * MAX_REL + 1
EPS = 1e-6
MASK_VALUE = -1e30
LOG2_E = math.log2(math.e)
Q_SCALE = LOG2_E / math.sqrt(HEAD_DIM)

LANES = 128
HEADS_PER_TILE = LANES // HEAD_DIM
N_PAIRS = N_HEADS // HEADS_PER_TILE
GROUP = LANES
GROUP_BAND = HALO + GROUP
SCORE_COLS = HEADS_PER_TILE * GROUP
BIAS_ROWS = GROUP + MAX_REL
SEQ_TILE = 1024
POOL_HALO = 16
OUT_COLS = 512
OUT_ROWS = 256
N_ATTN_PIECES = 4
LOOKAHEAD = 3
VMEM_LIMIT_BYTES = 60 * 1024 * 1024

_F32 = jnp.float32
_BF16 = jnp.bfloat16
_NT = (((1,), (1,)), ((), ()))


def _rms_norm(x, gain):
    return x * lax.rsqrt(jnp.mean(x * x, axis=-1, keepdims=True) + EPS) * gain


def _silu(x):
    return x / (1.0 + jnp.exp(-x))


PREP_ROWS = 128
_POOL_V, _POOL_G, _Q, _K, _V, _ATTN_G = range(6)


def _bias_table(rel_ref, out_ref):
    assert N_REL == LANES + 1
    j = lax.broadcasted_iota(jnp.int32, (BIAS_ROWS, GROUP), 0) + (GROUP_BAND - BIAS_ROWS)
    i = lax.broadcasted_iota(jnp.int32, (BIAS_ROWS, GROUP), 1)
    idx = jnp.clip(j - HALO - i, -MAX_REL, MAX_REL) + MAX_REL
    low = jnp.minimum(idx, LANES - 1)
    for head in range(N_HEADS):
        p, hh = divmod(head, HEADS_PER_TILE)
        row = jnp.broadcast_to(rel_ref[head:head + 1, 0:LANES], (BIAS_ROWS, LANES))
        last = jnp.broadcast_to(rel_ref[head:head + 1, LANES:LANES + 1], (BIAS_ROWS, LANES))
        first = jnp.broadcast_to(rel_ref[head:head + 1, 0:1], (BIAS_ROWS, LANES))
        table = jnp.where(idx == LANES, last, jnp.take_along_axis(row, low, axis=1))
        out_ref[p, :, hh * GROUP:(hh + 1) * GROUP] = (table - first) * LOG2_E


def _prep_kernel(w_in_ref, w_out_ref, pool_w_ref, rel_ref,
                 w_main_ref, w_qvt_ref, w_out_bf_ref, pool_w_bf_ref, bias_ref):
    def cols(part):
        return w_in_ref[:, part * POOL_WIDTH:(part + 1) * POOL_WIDTH]

    w_main_ref[...] = jnp.concatenate([cols(_POOL_V), cols(_POOL_G), cols(_K), cols(_ATTN_G)], axis=1).astype(_BF16)
    w_qvt_ref[...] = jnp.concatenate([cols(_Q), cols(_V)], axis=1).T.astype(_BF16)
    w_out_bf_ref[...] = w_out_ref[...].astype(_BF16)

    @pl.when(pl.program_id(0) == 0)
    def _():
        pool_w_bf_ref[...] = pool_w_ref[...].astype(_BF16)
        _bias_table(rel_ref, bias_ref)


def _prepare_params(w_in, w_out, pool_w, rel_bias):
    n_main = 4 * POOL_WIDTH
    n_qv = 2 * ATTN_WIDTH

    def whole(a):
        return pl.BlockSpec(a.shape, lambda i: (0,) * a.ndim)

    bias_shape = (N_PAIRS, BIAS_ROWS, SCORE_COLS)
    return pl.pallas_call(
        _prep_kernel,
        grid=(D_MODEL // PREP_ROWS,),
        in_specs=[
            pl.BlockSpec((PREP_ROWS, w_in.shape[1]), lambda i: (i, 0)),
            pl.BlockSpec((PREP_ROWS, D_MODEL), lambda i: (i, 0)),
            whole(pool_w),
            whole(rel_bias),
        ],
        out_specs=[
            pl.BlockSpec((PREP_ROWS, n_main), lambda i: (i, 0)),
            pl.BlockSpec((n_qv, PREP_ROWS), lambda i: (0, i)),
            pl.BlockSpec((PREP_ROWS, D_MODEL), lambda i: (i, 0)),
            pl.BlockSpec(pool_w.shape, lambda i: (0, 0, 0)),
            pl.BlockSpec(bias_shape, lambda i: (0, 0, 0)),
        ],
        out_shape=[
            jax.ShapeDtypeStruct((D_MODEL, n_main), _BF16),
            jax.ShapeDtypeStruct((n_qv, D_MODEL), _BF16),
            jax.ShapeDtypeStruct((POOL_WIDTH + ATTN_WIDTH, D_MODEL), _BF16),
            jax.ShapeDtypeStruct(pool_w.shape, _BF16),
            jax.ShapeDtypeStruct(bias_shape, _F32),
        ],
        compiler_params=pltpu.CompilerParams(dimension_semantics=("arbitrary",)),
        name="prepare_params",
    )(w_in, w_out, pool_w, rel_bias)


def _attend(groups, fillers, k_buf, ind_buf, vt_buf, qt_buf, gate_buf, bias_ref, y_buf, s_buf, m_buf):
    row = lax.broadcasted_iota(jnp.int32, (LANES, GROUP), 0)
    first_rows = row < HEAD_DIM
    lane = lax.broadcasted_iota(jnp.int32, (CHUNK, SCORE_COLS), 1)
    second_chunk = (lane & (GROUP - 1)) >= CHUNK
    pen_row = lax.broadcasted_iota(jnp.int32, (LANES, SCORE_COLS), 0)
    pen = jnp.where(pen_row == 0, MASK_VALUE, 0.0).astype(_BF16)
    items = [(g, p) for g in groups for p in range(N_PAIRS)]
    assert len(fillers) == len(items)

    def scores(i):
        g, p = items[i]
        lanes = slice(p * LANES, (p + 1) * LANES)
        band = slice(g * GROUP, g * GROUP + GROUP_BAND)
        lhs = jnp.concatenate([k_buf[band, lanes], ind_buf[band, :]], axis=1)
        qt = qt_buf[g, lanes, :]
        zero = jnp.zeros_like(qt)
        rhs = jnp.concatenate([jnp.where(first_rows, qt, zero), jnp.where(first_rows, zero, qt)], axis=1)
        rhs = jnp.concatenate([rhs, pen], axis=0)
        s = jnp.dot(lhs, rhs, preferred_element_type=_F32)
        lo = GROUP_BAND - BIAS_ROWS
        bias = bias_ref[p]
        s = jnp.concatenate([
            jnp.where(second_chunk, MASK_VALUE, s[:CHUNK]),
            s[CHUNK:lo],
            s[lo:-CHUNK] + bias[:BIAS_ROWS - CHUNK],
            jnp.where(second_chunk, s[-CHUNK:] + bias[BIAS_ROWS - CHUNK:], MASK_VALUE),
        ], axis=0)
        slot = i % (LOOKAHEAD + 1)
        s_buf[slot] = s
        m_buf[slot] = jnp.max(s, axis=0, keepdims=True)

    def finish(i):
        g, p = items[i]
        lanes = slice(p * LANES, (p + 1) * LANES)
        rows = slice(g * GROUP, (g + 1) * GROUP)
        slot = i % (LOOKAHEAD + 1)
        e = jnp.exp2(s_buf[slot] - m_buf[slot])
        l = jnp.sum(e, axis=0, keepdims=True)
        vt = jnp.concatenate([vt_buf[g + b, lanes, :] for b in range(GROUP_BAND // LANES)], axis=1)
        o = jnp.dot(vt, e.astype(_BF16), preferred_element_type=_F32)
        inv_l = 1.0 / l
        ot = jnp.concatenate([o[:HEAD_DIM, :GROUP] * inv_l[:, :GROUP],
                              o[HEAD_DIM:, GROUP:] * inv_l[:, GROUP:]], axis=0)
        gated = ot.T * gate_buf[rows, lanes]
        y_buf[rows, POOL_WIDTH + p * LANES:POOL_WIDTH + (p + 1) * LANES] = gated.astype(_BF16)

    for i in range(min(LOOKAHEAD, len(items))):
        scores(i)
    for i in range(len(items)):
        for filler in fillers[i]:
            filler()
        if i + LOOKAHEAD < len(items):
            scores(i + LOOKAHEAD)
        finish(i)


def _spread(pieces, n):
    return [pieces[(i * len(pieces)) // n:((i + 1) * len(pieces)) // n] for i in range(n)]


def _block_kernel(x_ref, g_in_ref, w_main_ref, w_qvt_ref, pool_w_ref, pool_scale_ref, bias_ref,
                  w_out_ref, g_out_ref, out_ref,
                  h_buf, k_buf, ind_buf, vt_buf, qt_buf, gate_buf, pgate_buf, y_buf, pv_buf, s_buf, m_buf):
    t = pl.program_id(1)
    tile = x_ref.shape[1]
    half = tile // 2
    n_groups = tile // GROUP
    halo_blocks = HALO // LANES

    @pl.when(t == 0)
    def _():
        pv_buf[0:POOL_HALO, :] = jnp.zeros((POOL_HALO, POOL_WIDTH), _F32)
        k_buf[0:HALO, :] = jnp.zeros((HALO, ATTN_WIDTH), _BF16)
        for b in range(halo_blocks):
            vt_buf[b] = jnp.zeros((ATTN_WIDTH, LANES), _BF16)
        lane = lax.broadcasted_iota(jnp.int32, (HALO, LANES), 1)
        ind_buf[0:HALO, :] = jnp.where(lane == 0, 1.0, 0.0).astype(_BF16)
        ind_buf[HALO:, :] = jnp.zeros((tile, LANES), _BF16)

    h_buf[...] = _rms_norm(x_ref[0], g_in_ref[...]).astype(_BF16)

    def proj_pieces(hf):
        rows = slice(hf * half, (hf + 1) * half)

        def main(part, rows=rows):
            lo = part * POOL_WIDTH
            return jnp.dot(h_buf[rows, :], w_main_ref[:, lo:lo + POOL_WIDTH], preferred_element_type=_F32)

        def pool_v():
            pv_buf[POOL_HALO + hf * half:POOL_HALO + (hf + 1) * half, :] = main(0)

        def pool_g():
            pgate_buf[rows, :] = _silu(main(1))

        def keys():
            for r0 in range(hf * half, (hf + 1) * half, OUT_ROWS):
                k_buf[HALO + r0:HALO + r0 + OUT_ROWS, :] = main(2, slice(r0, r0 + OUT_ROWS)).astype(_BF16)

        def attn_g():
            gate_buf[rows, :] = _silu(main(3))

        def transposed(part):
            w = w_qvt_ref[part * ATTN_WIDTH:(part + 1) * ATTN_WIDTH, :]
            return lax.dot_general(w, h_buf[rows, :], _NT, preferred_element_type=_F32)

        def queries():
            qt = transposed(0)
            for gg in range(half // GROUP):
                qt_buf[hf * (half // GROUP) + gg] = (qt[:, gg * GROUP:(gg + 1) * GROUP] * Q_SCALE).astype(_BF16)

        def values():
            vt = transposed(1)
            for gg in range(half // GROUP):
                vt_buf[halo_blocks + hf * (half // GROUP) + gg] = vt[:, gg * GROUP:(gg + 1) * GROUP].astype(_BF16)

        def pool(g):
            def run():
                w = POOL_WINDOWS[g]
                lanes = slice(g * POOL_GROUP_DIM, (g + 1) * POOL_GROUP_DIM)
                base = hf * half
                win = pv_buf[base:base + POOL_HALO + half, lanes]
                shift = 1
                while shift < w:
                    win = win + pltpu.roll(win, shift, 0)
                    shift *= 2
                win = win[POOL_HALO:]
                cur = pv_buf[POOL_HALO + base:POOL_HALO + base + half, lanes]
                head = POOL_HALO
                pos = t * tile + base + lax.broadcasted_iota(jnp.int32, (head, 1), 0)
                count = jnp.minimum(pos + 1, w).astype(_F32)
                mean = jnp.concatenate([win[:head] / count, win[head:] * (1.0 / w)], axis=0)
                d = (mean - cur).astype(_BF16)
                yg = jnp.dot(d, pool_w_ref[g], preferred_element_type=_F32)
                y_buf[rows, lanes] = (yg * pool_scale_ref[:, lanes] * pgate_buf[rows, lanes]).astype(_BF16)
            return run

        return [keys, queries, values, attn_g, pool_v, pool_g] + [pool(g) for g in range(len(POOL_WINDOWS))]

    def out_pieces(hf):
        def project(rows, c):
            def run():
                cols = slice(c * OUT_COLS, (c + 1) * OUT_COLS)
                out_ref[0, rows, cols] = x_ref[0, rows, cols] + jnp.dot(
                    y_buf[rows, :], w_out_ref[:, cols], preferred_element_type=_F32)
            return run

        def norm(rows):
            def run():
                out_ref[0, rows, :] = _rms_norm(out_ref[0, rows, :], g_out_ref[...])
            return run

        pieces = []
        for r0 in range(hf * half, (hf + 1) * half, OUT_ROWS):
            rows = slice(r0, r0 + OUT_ROWS)
            pieces += [project(rows, c) for c in range(D_MODEL // OUT_COLS)] + [norm(rows)]
        return pieces

    bufs = (k_buf, ind_buf, vt_buf, qt_buf, gate_buf, bias_ref, y_buf, s_buf, m_buf)
    groups_per_half = half // GROUP
    items_per_half = groups_per_half * N_PAIRS

    for piece in proj_pieces(0):
        piece()
    second = proj_pieces(1)
    _attend(list(range(n_groups)),
            _spread(second[:N_ATTN_PIECES], items_per_half - LOOKAHEAD + 1) + [[]] * (LOOKAHEAD - 1)
            + _spread(second[N_ATTN_PIECES:] + out_pieces(0), items_per_half), *bufs)
    for piece in out_pieces(1):
        piece()

    k_buf[0:HALO, :] = k_buf[tile:tile + HALO, :]
    for b in range(halo_blocks):
        vt_buf[b] = vt_buf[n_groups + b]
    pv_buf[0:POOL_HALO, :] = pv_buf[tile:tile + POOL_HALO, :]
    ind_buf[0:HALO, :] = jnp.zeros((HALO, LANES), _BF16)


def _fused_block(x, g_in, w_main, w_qvt, pool_w, pool_scale, bias, w_out, g_out, *, tile):
    batch, seq, d = x.shape
    assert d == D_MODEL and seq % tile == 0 and tile % (2 * GROUP) == 0 and tile >= HALO

    def resident(shape):
        return pl.BlockSpec(shape, lambda b, t: (0,) * len(shape), pipeline_mode=pl.Buffered(1))

    return pl.pallas_call(
        _block_kernel,
        grid=(batch, seq // tile),
        in_specs=[
            pl.BlockSpec((1, tile, D_MODEL), lambda b, t: (b, t, 0)),
            resident((1, D_MODEL)),
            resident(w_main.shape),
            resident(w_qvt.shape),
            resident(pool_w.shape),
            resident((1, POOL_WIDTH)),
            resident(bias.shape),
            resident(w_out.shape),
            resident((1, D_MODEL)),
        ],
        out_specs=pl.BlockSpec((1, tile, D_MODEL), lambda b, t: (b, t, 0)),
        out_shape=jax.ShapeDtypeStruct(x.shape, x.dtype),
        scratch_shapes=[
            pltpu.VMEM((tile, D_MODEL), _BF16),
            pltpu.VMEM((HALO + tile, ATTN_WIDTH), _BF16),
            pltpu.VMEM((HALO + tile, LANES), _BF16),
            pltpu.VMEM(((HALO + tile) // LANES, ATTN_WIDTH, LANES), _BF16),
            pltpu.VMEM((tile // GROUP, ATTN_WIDTH, GROUP), _BF16),
            pltpu.VMEM((tile, ATTN_WIDTH), _F32),
            pltpu.VMEM((tile, POOL_WIDTH), _F32),
            pltpu.VMEM((tile, POOL_WIDTH + ATTN_WIDTH), _BF16),
            pltpu.VMEM((POOL_HALO + tile, POOL_WIDTH), _F32),
            pltpu.VMEM((LOOKAHEAD + 1, GROUP_BAND, SCORE_COLS), _F32),
            pltpu.VMEM((LOOKAHEAD + 1, 1, SCORE_COLS), _F32),
        ],
        compiler_params=pltpu.CompilerParams(
            dimension_semantics=("arbitrary", "arbitrary"),
            vmem_limit_bytes=VMEM_LIMIT_BYTES,
        ),
        name="hybrid_block",
    )(x, g_in, w_main, w_qvt, pool_w, pool_scale, bias, w_out, g_out)


@jax.jit
def kernel(x, norm_gain, w_in, pool_w, pool_scale, rel_bias, w_out, final_norm_gain):
    w_main, w_qvt, w_out_bf, pool_w_bf, bias = _prepare_params(w_in[0], w_out[0], pool_w[0], rel_bias[0])
    return _fused_block(
        x,
        norm_gain[0][None, :],
        w_main,
        w_qvt,
        pool_w_bf,
        pool_scale[0][None, :],
        bias,
        w_out_bf,
        final_norm_gain[None, :],
        tile=SEQ_TILE,
    )
```

```python
import math

import jax
import jax.numpy as jnp
from jax import lax
from jax.experimental import pallas as pl
from jax.experimental.pallas import tpu as pltpu

D_MODEL = 1024
CHUNK = 64
POOL_WIDTH = 512
ATTN_WIDTH = 512
POOL_WINDOWS = (2, 4, 8, 16)
POOL_GROUP_DIM = 128
HEAD_DIM = 64
N_HEADS = 8
LEFT_CHUNKS = 8
HALO = LEFT_CHUNKS * CHUNK
MAX_REL = 64
N_REL = 2 * MAX_REL + 1
EPS = 1e-6
MASK_VALUE = -1e30
LOG2_E = math.log2(math.e)
Q_SCALE = LOG2_E / math.sqrt(HEAD_DIM)

LANES = 128
HEADS_PER_TILE = LANES // HEAD_DIM
N_PAIRS = N_HEADS // HEADS_PER_TILE
GROUP = LANES
GROUP_BAND = HALO + GROUP
SCORE_COLS = HEADS_PER_TILE * GROUP
BIAS_ROWS = GROUP + MAX_REL
SEQ_TILE = 1024
POOL_HALO = 16
OUT_COLS = 256
OUT_ROWS = 256
N_ATTN_PIECES = 4
LOOKAHEAD = 3
VMEM_LIMIT_BYTES = 60 * 1024 * 1024

_F32 = jnp.float32
_BF16 = jnp.bfloat16
_NT = (((1,), (1,)), ((), ()))


def _rms_norm(x, gain):
    return x * lax.rsqrt(jnp.mean(x * x, axis=-1, keepdims=True) + EPS) * gain


def _silu(x):
    return x / (1.0 + jnp.exp(-x))


PREP_ROWS = 128
_POOL_V, _POOL_G, _Q, _K, _V, _ATTN_G = range(6)


def _bias_table(rel_ref, out_ref):
    assert N_REL == LANES + 1
    j = lax.broadcasted_iota(jnp.int32, (BIAS_ROWS, GROUP), 0) + (GROUP_BAND - BIAS_ROWS)
    i = lax.broadcasted_iota(jnp.int32, (BIAS_ROWS, GROUP), 1)
    idx = jnp.clip(j - HALO - i, -MAX_REL, MAX_REL) + MAX_REL
    low = jnp.minimum(idx, LANES - 1)
    for head in range(N_HEADS):
        p, hh = divmod(head, HEADS_PER_TILE)
        row = jnp.broadcast_to(rel_ref[head:head + 1, 0:LANES], (BIAS_ROWS, LANES))
        last = jnp.broadcast_to(rel_ref[head:head + 1, LANES:LANES + 1], (BIAS_ROWS, LANES))
        first = jnp.broadcast_to(rel_ref[head:head + 1, 0:1], (BIAS_ROWS, LANES))
        table = jnp.where(idx == LANES, last, jnp.take_along_axis(row, low, axis=1))
        out_ref[p, :, hh * GROUP:(hh + 1) * GROUP] = (table - first) * LOG2_E


def _prep_kernel(w_in_ref, w_out_ref, pool_w_ref, rel_ref,
                 w_main_ref, w_qvt_ref, w_out_bf_ref, pool_w_bf_ref, bias_ref):
    def cols(part):
        return w_in_ref[:, part * POOL_WIDTH:(part + 1) * POOL_WIDTH]

    w_main_ref[...] = jnp.concatenate([cols(_POOL_V), cols(_POOL_G), cols(_K), cols(_ATTN_G)], axis=1).astype(_BF16)
    w_qvt_ref[...] = jnp.concatenate([cols(_Q), cols(_V)], axis=1).T.astype(_BF16)
    w_out_bf_ref[...] = w_out_ref[...].astype(_BF16)

    @pl.when(pl.program_id(0) == 0)
    def _():
        pool_w_bf_ref[...] = pool_w_ref[...].astype(_BF16)
        _bias_table(rel_ref, bias_ref)


def _prepare_params(w_in, w_out, pool_w, rel_bias):
    n_main = 4 * POOL_WIDTH
    n_qv = 2 * ATTN_WIDTH

    def whole(a):
        return pl.BlockSpec(a.shape, lambda i: (0,) * a.ndim)

    bias_shape = (N_PAIRS, BIAS_ROWS, SCORE_COLS)
    return pl.pallas_call(
        _prep_kernel,
        grid=(D_MODEL // PREP_ROWS,),
        in_specs=[
            pl.BlockSpec((PREP_ROWS, w_in.shape[1]), lambda i: (i, 0)),
            pl.BlockSpec((PREP_ROWS, D_MODEL), lambda i: (i, 0)),
            whole(pool_w),
            whole(rel_bias),
        ],
        out_specs=[
            pl.BlockSpec((PREP_ROWS, n_main), lambda i: (i, 0)),
            pl.BlockSpec((n_qv, PREP_ROWS), lambda i: (0, i)),
            pl.BlockSpec((PREP_ROWS, D_MODEL), lambda i: (i, 0)),
            pl.BlockSpec(pool_w.shape, lambda i: (0, 0, 0)),
            pl.BlockSpec(bias_shape, lambda i: (0, 0, 0)),
        ],
        out_shape=[
            jax.ShapeDtypeStruct((D_MODEL, n_main), _BF16),
            jax.ShapeDtypeStruct((n_qv, D_MODEL), _BF16),
            jax.ShapeDtypeStruct((POOL_WIDTH + ATTN_WIDTH, D_MODEL), _BF16),
            jax.ShapeDtypeStruct(pool_w.shape, _BF16),
            jax.ShapeDtypeStruct(bias_shape, _F32),
        ],
        compiler_params=pltpu.CompilerParams(dimension_semantics=("arbitrary",)),
        name="prepare_params",
    )(w_in, w_out, pool_w, rel_bias)


def _attend(groups, fillers, k_buf, ind_buf, vt_buf, qt_buf, gate_buf, bias_ref, y_buf, s_buf, m_buf):
    row = lax.broadcasted_iota(jnp.int32, (LANES, GROUP), 0)
    first_rows = row < HEAD_DIM
    lane = lax.broadcasted_iota(jnp.int32, (CHUNK, SCORE_COLS), 1)
    second_chunk = (lane & (GROUP - 1)) >= CHUNK
    pen_row = lax.broadcasted_iota(jnp.int32, (LANES, SCORE_COLS), 0)
    pen = jnp.where(pen_row == 0, MASK_VALUE, 0.0).astype(_BF16)
    items = [(g, p) for g in groups for p in range(N_PAIRS)]
    assert len(fillers) == len(items)

    def scores(i):
        g, p = items[i]
        lanes = slice(p * LANES, (p + 1) * LANES)
        band = slice(g * GROUP, g * GROUP + GROUP_BAND)
        lhs = k_buf[band, lanes]
        qt = qt_buf[g, lanes, :]
        zero = jnp.zeros_like(qt)
        rhs = jnp.concatenate([jnp.where(first_rows, qt, zero), jnp.where(first_rows, zero, qt)], axis=1)
        if g * GROUP < HALO:
            lhs = jnp.concatenate([lhs, ind_buf[band, :]], axis=1)
            rhs = jnp.concatenate([rhs, pen], axis=0)
        mid = GROUP_BAND // 2
        lo = GROUP_BAND - BIAS_ROWS
        bias = bias_ref[p]
        s0 = jnp.dot(lhs[:mid], rhs, preferred_element_type=_F32)
        s1 = jnp.dot(lhs[mid:], rhs, preferred_element_type=_F32)
        s0 = jnp.concatenate([jnp.where(second_chunk, MASK_VALUE, s0[:CHUNK]), s0[CHUNK:]], axis=0)
        s1 = jnp.concatenate([
            s1[:lo - mid],
            s1[lo - mid:-CHUNK] + bias[:BIAS_ROWS - CHUNK],
            jnp.where(second_chunk, s1[-CHUNK:] + bias[BIAS_ROWS - CHUNK:], MASK_VALUE),
        ], axis=0)
        slot = i % (LOOKAHEAD + 1)
        s_buf[slot, 0:mid, :] = s0
        s_buf[slot, mid:, :] = s1
        m_buf[slot] = jnp.maximum(jnp.max(s0, axis=0, keepdims=True), jnp.max(s1, axis=0, keepdims=True))

    def finish(i):
        g, p = items[i]
        lanes = slice(p * LANES, (p + 1) * LANES)
        rows = slice(g * GROUP, (g + 1) * GROUP)
        slot = i % (LOOKAHEAD + 1)
        e = jnp.exp2(s_buf[slot] - m_buf[slot])
        l = jnp.sum(e, axis=0, keepdims=True)
        vt = jnp.concatenate([vt_buf[g + b, lanes, :] for b in range(GROUP_BAND // LANES)], axis=1)
        o = jnp.dot(vt, e.astype(_BF16), preferred_element_type=_F32)
        inv_l = 1.0 / l
        ot = jnp.concatenate([o[:HEAD_DIM, :GROUP] * inv_l[:, :GROUP],
                              o[HEAD_DIM:, GROUP:] * inv_l[:, GROUP:]], axis=0)
        gated = ot.T * gate_buf[rows, lanes]
        y_buf[rows, POOL_WIDTH + p * LANES:POOL_WIDTH + (p + 1) * LANES] = gated.astype(_BF16)

    for i in range(min(LOOKAHEAD, len(items))):
        scores(i)
    for i in range(len(items)):
        for filler in fillers[i]:
            filler()
        if i + LOOKAHEAD < len(items):
            scores(i + LOOKAHEAD)
        finish(i)


def _spread(pieces, n):
    return [pieces[(i * len(pieces)) // n:((i + 1) * len(pieces)) // n] for i in range(n)]


def _block_kernel(x_ref, g_in_ref, w_main_ref, w_qvt_ref, pool_w_ref, pool_scale_ref, bias_ref,
                  w_out_ref, g_out_ref, out_ref,
                  h_buf, k_buf, ind_buf, vt_buf, qt_buf, gate_buf, pgate_buf, y_buf, pv_buf, s_buf, m_buf):
    t = pl.program_id(1)
    tile = x_ref.shape[1]
    half = tile // 2
    n_groups = tile // GROUP
    halo_blocks = HALO // LANES

    @pl.when(t == 0)
    def _():
        pv_buf[0:POOL_HALO, :] = jnp.zeros((POOL_HALO, POOL_WIDTH), _F32)
        k_buf[0:HALO, :] = jnp.zeros((HALO, ATTN_WIDTH), _BF16)
        for b in range(halo_blocks):
            vt_buf[b] = jnp.zeros((ATTN_WIDTH, LANES), _BF16)
        lane = lax.broadcasted_iota(jnp.int32, (HALO, LANES), 1)
        ind_buf[0:HALO, :] = jnp.where(lane == 0, 1.0, 0.0).astype(_BF16)
        ind_buf[HALO:, :] = jnp.zeros((tile, LANES), _BF16)

    h_buf[...] = _rms_norm(x_ref[0], g_in_ref[...]).astype(_BF16)

    def proj_pieces(hf):
        rows = slice(hf * half, (hf + 1) * half)

        def main(part, rows=rows):
            lo = part * POOL_WIDTH
            return jnp.dot(h_buf[rows, :], w_main_ref[:, lo:lo + POOL_WIDTH], preferred_element_type=_F32)

        def pool_v():
            pv_buf[POOL_HALO + hf * half:POOL_HALO + (hf + 1) * half, :] = main(0)

        def pool_g():
            pgate_buf[rows, :] = _silu(main(1))

        def keys():
            for r0 in range(hf * half, (hf + 1) * half, OUT_ROWS):
                k_buf[HALO + r0:HALO + r0 + OUT_ROWS, :] = main(2, slice(r0, r0 + OUT_ROWS)).astype(_BF16)

        def attn_g():
            gate_buf[rows, :] = _silu(main(3))

        def transposed(part):
            w = w_qvt_ref[part * ATTN_WIDTH:(part + 1) * ATTN_WIDTH, :]
            return lax.dot_general(w, h_buf[rows, :], _NT, preferred_element_type=_F32)

        def queries():
            qt = transposed(0)
            for gg in range(half // GROUP):
                qt_buf[hf * (half // GROUP) + gg] = (qt[:, gg * GROUP:(gg + 1) * GROUP] * Q_SCALE).astype(_BF16)

        def values():
            vt = transposed(1)
            for gg in range(half // GROUP):
                vt_buf[halo_blocks + hf * (half // GROUP) + gg] = vt[:, gg * GROUP:(gg + 1) * GROUP].astype(_BF16)

        def pool(g):
            def run():
                w = POOL_WINDOWS[g]
                lanes = slice(g * POOL_GROUP_DIM, (g + 1) * POOL_GROUP_DIM)
                base = hf * half
                win = pv_buf[base:base + POOL_HALO + half, lanes]
                shift = 1
                while shift < w:
                    win = win + pltpu.roll(win, shift, 0)
                    shift *= 2
                win = win[POOL_HALO:]
                cur = pv_buf[POOL_HALO + base:POOL_HALO + base + half, lanes]
                head = POOL_HALO
                pos = t * tile + base + lax.broadcasted_iota(jnp.int32, (head, 1), 0)
                count = jnp.minimum(pos + 1, w).astype(_F32)
                mean = jnp.concatenate([win[:head] / count, win[head:] * (1.0 / w)], axis=0)
                d = (mean - cur).astype(_BF16)
                yg = jnp.dot(d, pool_w_ref[g], preferred_element_type=_F32)
                y_buf[rows, lanes] = (yg * pool_scale_ref[:, lanes] * pgate_buf[rows, lanes]).astype(_BF16)
            return run

        return [keys, queries, values, attn_g, pool_v, pool_g] + [pool(g) for g in range(len(POOL_WINDOWS))]

    def out_pieces(hf):
        def project(rows, c):
            def run():
                cols = slice(c * OUT_COLS, (c + 1) * OUT_COLS)
                out_ref[0, rows, cols] = x_ref[0, rows, cols] + jnp.dot(
                    y_buf[rows, :], w_out_ref[:, cols], preferred_element_type=_F32)
            return run

        def norm(rows):
            def run():
                out_ref[0, rows, :] = _rms_norm(out_ref[0, rows, :], g_out_ref[...])
            return run

        pieces = []
        for r0 in range(hf * half, (hf + 1) * half, OUT_ROWS):
            rows = slice(r0, r0 + OUT_ROWS)
            pieces += [project(rows, c) for c in range(D_MODEL // OUT_COLS)] + [norm(rows)]
        return pieces

    bufs = (k_buf, ind_buf, vt_buf, qt_buf, gate_buf, bias_ref, y_buf, s_buf, m_buf)
    groups_per_half = half // GROUP
    items_per_half = groups_per_half * N_PAIRS

    for piece in proj_pieces(0):
        piece()
    second = proj_pieces(1)
    _attend(list(range(n_groups)),
            _spread(second[:N_ATTN_PIECES], items_per_half - LOOKAHEAD + 1) + [[]] * (LOOKAHEAD - 1)
            + _spread(second[N_ATTN_PIECES:] + out_pieces(0), items_per_half), *bufs)
    for piece in out_pieces(1):
        piece()

    k_buf[0:HALO, :] = k_buf[tile:tile + HALO, :]
    for b in range(halo_blocks):
        vt_buf[b] = vt_buf[n_groups + b]
    pv_buf[0:POOL_HALO, :] = pv_buf[tile:tile + POOL_HALO, :]
    ind_buf[0:HALO, :] = jnp.zeros((HALO, LANES), _BF16)


def _fused_block(x, g_in, w_main, w_qvt, pool_w, pool_scale, bias, w_out, g_out, *, tile):
    batch, seq, d = x.shape
    assert d == D_MODEL and seq % tile == 0 and tile % (2 * GROUP) == 0 and tile >= HALO

    def resident(shape):
        return pl.BlockSpec(shape, lambda b, t: (0,) * len(shape), pipeline_mode=pl.Buffered(1))

    return pl.pallas_call(
        _block_kernel,
        grid=(batch, seq // tile),
        in_specs=[
            pl.BlockSpec((1, tile, D_MODEL), lambda b, t: (b, t, 0)),
            resident((1, D_MODEL)),
            resident(w_main.shape),
            resident(w_qvt.shape),
            resident(pool_w.shape),
            resident((1, POOL_WIDTH)),
            resident(bias.shape),
            resident(w_out.shape),
            resident((1, D_MODEL)),
        ],
        out_specs=pl.BlockSpec((1, tile, D_MODEL), lambda b, t: (b, t, 0)),
        out_shape=jax.ShapeDtypeStruct(x.shape, x.dtype),
        scratch_shapes=[
            pltpu.VMEM((tile, D_MODEL), _BF16),
            pltpu.VMEM((HALO + tile, ATTN_WIDTH), _BF16),
            pltpu.VMEM((HALO + tile, LANES), _BF16),
            pltpu.VMEM(((HALO + tile) // LANES, ATTN_WIDTH, LANES), _BF16),
            pltpu.VMEM((tile // GROUP, ATTN_WIDTH, GROUP), _BF16),
            pltpu.VMEM((tile, ATTN_WIDTH), _F32),
            pltpu.VMEM((tile, POOL_WIDTH), _F32),
            pltpu.VMEM((tile, POOL_WIDTH + ATTN_WIDTH), _BF16),
            pltpu.VMEM((POOL_HALO + tile, POOL_WIDTH), _F32),
            pltpu.VMEM((LOOKAHEAD + 1, GROUP_BAND, SCORE_COLS), _F32),
            pltpu.VMEM((LOOKAHEAD + 1, 1, SCORE_COLS), _F32),
        ],
        compiler_params=pltpu.CompilerParams(
            dimension_semantics=("arbitrary", "arbitrary"),
            vmem_limit_bytes=VMEM_LIMIT_BYTES,
        ),
        name="hybrid_block",
    )(x, g_in, w_main, w_qvt, pool_w, pool_scale, bias, w_out, g_out)


@jax.jit
def kernel(x, norm_gain, w_in, pool_w, pool_scale, rel_bias, w_out, final_norm_gain):
    w_main, w_qvt, w_out_bf, pool_w_bf, bias = _prepare_params(w_in[0], w_out[0], pool_w[0], rel_bias[0])
    return _fused_block(
        x,
        norm_gain[0][None, :],
        w_main,
        w_qvt,
        pool_w_bf,
        pool_scale[0][None, :],
        bias,
        w_out_bf,
        final_norm_gain[None, :],
        tile=SEQ_TILE,
    )
```

```python
import math

import jax
import jax.numpy as jnp
from jax import lax
from jax.experimental import pallas as pl
from jax.experimental.pallas import tpu as pltpu

D_MODEL = 1024
CHUNK = 64
POOL_WIDTH = 512
ATTN_WIDTH = 512
POOL_WINDOWS = (2, 4, 8, 16)
POOL_GROUP_DIM = 128
HEAD_DIM = 64
N_HEADS = 8
LEFT_CHUNKS = 8
HALO = LEFT_CHUNKS * CHUNK
MAX_REL = 64
N_REL = 2 * MAX_REL + 1
EPS = 1e-6
MASK_VALUE = -1e30
LOG2_E = math.log2(math.e)
Q_SCALE = LOG2_E / math.sqrt(HEAD_DIM)

LANES = 128
HEADS_PER_TILE = LANES // HEAD_DIM
N_PAIRS = N_HEADS // HEADS_PER_TILE
GROUP = LANES
GROUP_BAND = HALO + GROUP
SCORE_COLS = HEADS_PER_TILE * GROUP
BIAS_ROWS = GROUP + MAX_REL
SEQ_TILE = 1024
POOL_HALO = 16
OUT_COLS = 256
OUT_ROWS = 256
N_ATTN_PIECES = 4
LOOKAHEAD = 3
VMEM_LIMIT_BYTES = 60 * 1024 * 1024

_F32 = jnp.float32
_BF16 = jnp.bfloat16
_NT = (((1,), (1,)), ((), ()))


def _rms_norm(x, gain):
    return x * lax.rsqrt(jnp.mean(x * x, axis=-1, keepdims=True) + EPS) * gain


def _silu(x):
    return x / (1.0 + jnp.exp(-x))


PREP_ROWS = 128
_POOL_V, _POOL_G, _Q, _K, _V, _ATTN_G = range(6)


def _bias_table(rel_ref, out_ref):
    assert N_REL == LANES + 1
    j = lax.broadcasted_iota(jnp.int32, (BIAS_ROWS, GROUP), 0) + (GROUP_BAND - BIAS_ROWS)
    i = lax.broadcasted_iota(jnp.int32, (BIAS_ROWS, GROUP), 1)
    idx = jnp.clip(j - HALO - i, -MAX_REL, MAX_REL) + MAX_REL
    low = jnp.minimum(idx, LANES - 1)
    for head in range(N_HEADS):
        p, hh = divmod(head, HEADS_PER_TILE)
        row = jnp.broadcast_to(rel_ref[head:head + 1, 0:LANES], (BIAS_ROWS, LANES))
        last = jnp.broadcast_to(rel_ref[head:head + 1, LANES:LANES + 1], (BIAS_ROWS, LANES))
        first = jnp.broadcast_to(rel_ref[head:head + 1, 0:1], (BIAS_ROWS, LANES))
        table = jnp.where(idx == LANES, last, jnp.take_along_axis(row, low, axis=1))
        out_ref[p, :, hh * GROUP:(hh + 1) * GROUP] = (table - first) * LOG2_E


def _prep_kernel(w_in_ref, w_out_ref, pool_w_ref, rel_ref,
                 w_main_ref, w_qvt_ref, w_out_bf_ref, pool_w_bf_ref, bias_ref):
    def cols(part):
        return w_in_ref[:, part * POOL_WIDTH:(part + 1) * POOL_WIDTH]

    w_main_ref[...] = jnp.concatenate([cols(_POOL_V), cols(_POOL_G), cols(_K), cols(_ATTN_G)], axis=1).astype(_BF16)
    w_qvt_ref[...] = jnp.concatenate([cols(_Q), cols(_V)], axis=1).T.astype(_BF16)
    w_out_bf_ref[...] = w_out_ref[...].astype(_BF16)

    @pl.when(pl.program_id(0) == 0)
    def _():
        pool_w_bf_ref[...] = pool_w_ref[...].astype(_BF16)
        _bias_table(rel_ref, bias_ref)


def _prepare_params(w_in, w_out, pool_w, rel_bias):
    n_main = 4 * POOL_WIDTH
    n_qv = 2 * ATTN_WIDTH

    def whole(a):
        return pl.BlockSpec(a.shape, lambda i: (0,) * a.ndim)

    bias_shape = (N_PAIRS, BIAS_ROWS, SCORE_COLS)
    return pl.pallas_call(
        _prep_kernel,
        grid=(D_MODEL // PREP_ROWS,),
        in_specs=[
            pl.BlockSpec((PREP_ROWS, w_in.shape[1]), lambda i: (i, 0)),
            pl.BlockSpec((PREP_ROWS, D_MODEL), lambda i: (i, 0)),
            whole(pool_w),
            whole(rel_bias),
        ],
        out_specs=[
            pl.BlockSpec((PREP_ROWS, n_main), lambda i: (i, 0)),
            pl.BlockSpec((n_qv, PREP_ROWS), lambda i: (0, i)),
            pl.BlockSpec((PREP_ROWS, D_MODEL), lambda i: (i, 0)),
            pl.BlockSpec(pool_w.shape, lambda i: (0, 0, 0)),
            pl.BlockSpec(bias_shape, lambda i: (0, 0, 0)),
        ],
        out_shape=[
            jax.ShapeDtypeStruct((D_MODEL, n_main), _BF16),
            jax.ShapeDtypeStruct((n_qv, D_MODEL), _BF16),
            jax.ShapeDtypeStruct((POOL_WIDTH + ATTN_WIDTH, D_MODEL), _BF16),
            jax.ShapeDtypeStruct(pool_w.shape, _BF16),
            jax.ShapeDtypeStruct(bias_shape, _F32),
        ],
        compiler_params=pltpu.CompilerParams(dimension_semantics=("arbitrary",)),
        name="prepare_params",
    )(w_in, w_out, pool_w, rel_bias)


def _attend(groups, fillers, start_penalty, k_buf, vt_buf, qt_buf, gate_buf, bias_ref, y_buf, s_buf, m_buf):
    row = lax.broadcasted_iota(jnp.int32, (LANES, GROUP), 0)
    first_rows = row < HEAD_DIM
    lane = lax.broadcasted_iota(jnp.int32, (CHUNK, SCORE_COLS), 1)
    second_chunk = (lane & (GROUP - 1)) >= CHUNK
    items = [(g, p) for g in groups for p in range(N_PAIRS)]
    assert len(fillers) == len(items)

    def scores(i):
        g, p = items[i]
        lanes = slice(p * LANES, (p + 1) * LANES)
        band = slice(g * GROUP, g * GROUP + GROUP_BAND)
        lhs = k_buf[band, lanes]
        qt = qt_buf[g, lanes, :]
        zero = jnp.zeros_like(qt)
        rhs = jnp.concatenate([jnp.where(first_rows, qt, zero), jnp.where(first_rows, zero, qt)], axis=1)
        mid = GROUP_BAND // 2
        halves = [jnp.dot(lhs[:mid], rhs, preferred_element_type=_F32),
                  jnp.dot(lhs[mid:], rhs, preferred_element_type=_F32)]
        lo = GROUP_BAND - BIAS_ROWS
        hist = max(HALO - g * GROUP, 0)
        slot = i % (LOOKAHEAD + 1)
        m = None
        for r0, sh in zip((0, mid), halves):
            parts = []
            for c0 in range(r0, r0 + mid, CHUNK):
                sc = sh[c0 - r0:c0 - r0 + CHUNK]
                if c0 < hist:
                    sc = sc + start_penalty
                if c0 >= lo:
                    sc = sc + bias_ref[p, c0 - lo:c0 - lo + CHUNK, :]
                if c0 == 0:
                    sc = jnp.where(second_chunk, MASK_VALUE, sc)
                if c0 == GROUP_BAND - CHUNK:
                    sc = jnp.where(second_chunk, sc, MASK_VALUE)
                parts.append(sc)
            sh = jnp.concatenate(parts, axis=0)
            s_buf[slot, r0:r0 + mid, :] = sh
            mh = jnp.max(sh, axis=0, keepdims=True)
            m = mh if m is None else jnp.maximum(m, mh)
        m_buf[slot] = m

    def finish(i):
        g, p = items[i]
        lanes = slice(p * LANES, (p + 1) * LANES)
        rows = slice(g * GROUP, (g + 1) * GROUP)
        slot = i % (LOOKAHEAD + 1)
        e = jnp.exp2(s_buf[slot] - m_buf[slot])
        l = jnp.sum(e, axis=0, keepdims=True)
        vt = jnp.concatenate([vt_buf[g + b, lanes, :] for b in range(GROUP_BAND // LANES)], axis=1)
        o = jnp.dot(vt, e.astype(_BF16), preferred_element_type=_F32)
        inv_l = 1.0 / l
        ot = jnp.concatenate([o[:HEAD_DIM, :GROUP] * inv_l[:, :GROUP],
                              o[HEAD_DIM:, GROUP:] * inv_l[:, GROUP:]], axis=0)
        gated = ot.T * gate_buf[rows, lanes]
        y_buf[rows, POOL_WIDTH + p * LANES:POOL_WIDTH + (p + 1) * LANES] = gated.astype(_BF16)

    for i in range(min(LOOKAHEAD, len(items))):
        scores(i)
    for i in range(len(items)):
        for filler in fillers[i]:
            filler()
        if i + LOOKAHEAD < len(items):
            scores(i + LOOKAHEAD)
        finish(i)


def _spread(pieces, n):
    return [pieces[(i * len(pieces)) // n:((i + 1) * len(pieces)) // n] for i in range(n)]


def _block_kernel(x_ref, g_in_ref, w_main_ref, w_qvt_ref, pool_w_ref, pool_scale_ref, bias_ref,
                  w_out_ref, g_out_ref, out_ref,
                  h_buf, k_buf, vt_buf, qt_buf, gate_buf, pgate_buf, y_buf, pv_buf, s_buf, m_buf):
    t = pl.program_id(1)
    tile = x_ref.shape[1]
    half = tile // 2
    n_groups = tile // GROUP
    halo_blocks = HALO // LANES

    @pl.when(t == 0)
    def _():
        pv_buf[0:POOL_HALO, :] = jnp.zeros((POOL_HALO, POOL_WIDTH), _F32)
        k_buf[0:HALO, :] = jnp.zeros((HALO, ATTN_WIDTH), _BF16)
        for b in range(halo_blocks):
            vt_buf[b] = jnp.zeros((ATTN_WIDTH, LANES), _BF16)

    h_buf[...] = _rms_norm(x_ref[0], g_in_ref[...]).astype(_BF16)

    def proj_pieces(hf):
        rows = slice(hf * half, (hf + 1) * half)

        def main(part, rows=rows):
            lo = part * POOL_WIDTH
            return jnp.dot(h_buf[rows, :], w_main_ref[:, lo:lo + POOL_WIDTH], preferred_element_type=_F32)

        def pool_v():
            pv_buf[POOL_HALO + hf * half:POOL_HALO + (hf + 1) * half, :] = main(0)

        def pool_g():
            pgate_buf[rows, :] = _silu(main(1))

        def keys():
            for r0 in range(hf * half, (hf + 1) * half, OUT_ROWS):
                k_buf[HALO + r0:HALO + r0 + OUT_ROWS, :] = main(2, slice(r0, r0 + OUT_ROWS)).astype(_BF16)

        def attn_g():
            gate_buf[rows, :] = _silu(main(3))

        def transposed(part):
            w = w_qvt_ref[part * ATTN_WIDTH:(part + 1) * ATTN_WIDTH, :]
            return lax.dot_general(w, h_buf[rows, :], _NT, preferred_element_type=_F32)

        def queries():
            qt = transposed(0)
            for gg in range(half // GROUP):
                qt_buf[hf * (half // GROUP) + gg] = (qt[:, gg * GROUP:(gg + 1) * GROUP] * Q_SCALE).astype(_BF16)

        def values():
            vt = transposed(1)
            for gg in range(half // GROUP):
                vt_buf[halo_blocks + hf * (half // GROUP) + gg] = vt[:, gg * GROUP:(gg + 1) * GROUP].astype(_BF16)

        def pool(g):
            def run():
                w = POOL_WINDOWS[g]
                lanes = slice(g * POOL_GROUP_DIM, (g + 1) * POOL_GROUP_DIM)
                base = hf * half
                win = pv_buf[base:base + POOL_HALO + half, lanes]
                shift = 1
                while shift < w:
                    win = win + pltpu.roll(win, shift, 0)
                    shift *= 2
                win = win[POOL_HALO:]
                cur = pv_buf[POOL_HALO + base:POOL_HALO + base + half, lanes]
                head = POOL_HALO
                pos = t * tile + base + lax.broadcasted_iota(jnp.int32, (head, 1), 0)
                count = jnp.minimum(pos + 1, w).astype(_F32)
                mean = jnp.concatenate([win[:head] / count, win[head:] * (1.0 / w)], axis=0)
                d = (mean - cur).astype(_BF16)
                yg = jnp.dot(d, pool_w_ref[g], preferred_element_type=_F32)
                y_buf[rows, lanes] = (yg * pool_scale_ref[:, lanes] * pgate_buf[rows, lanes]).astype(_BF16)
            return run

        return [keys, queries, values, attn_g, pool_v, pool_g] + [pool(g) for g in range(len(POOL_WINDOWS))]

    def out_pieces(hf):
        def project(rows, c):
            def run():
                cols = slice(c * OUT_COLS, (c + 1) * OUT_COLS)
                out_ref[0, rows, cols] = x_ref[0, rows, cols] + jnp.dot(
                    y_buf[rows, :], w_out_ref[:, cols], preferred_element_type=_F32)
            return run

        def norm(rows):
            def run():
                out_ref[0, rows, :] = _rms_norm(out_ref[0, rows, :], g_out_ref[...])
            return run

        pieces = []
        for r0 in range(hf * half, (hf + 1) * half, OUT_ROWS):
            rows = slice(r0, r0 + OUT_ROWS)
            pieces += [project(rows, c) for c in range(D_MODEL // OUT_COLS)] + [norm(rows)]
        return pieces

    bufs = (k_buf, vt_buf, qt_buf, gate_buf, bias_ref, y_buf, s_buf, m_buf)
    groups_per_half = half // GROUP
    items_per_half = groups_per_half * N_PAIRS

    for piece in proj_pieces(0):
        piece()
    second = proj_pieces(1)
    start_penalty = jnp.where(t == 0, MASK_VALUE, 0.0).astype(_F32)
    _attend(list(range(n_groups)),
            _spread(second[:N_ATTN_PIECES], items_per_half - LOOKAHEAD + 1) + [[]] * (LOOKAHEAD - 1)
            + _spread(second[N_ATTN_PIECES:] + out_pieces(0), items_per_half), start_penalty, *bufs)
    for piece in out_pieces(1):
        piece()

    k_buf[0:HALO, :] = k_buf[tile:tile + HALO, :]
    for b in range(halo_blocks):
        vt_buf[b] = vt_buf[n_groups + b]
    pv_buf[0:POOL_HALO, :] = pv_buf[tile:tile + POOL_HALO, :]


def _fused_block(x, g_in, w_main, w_qvt, pool_w, pool_scale, bias, w_out, g_out, *, tile):
    batch, seq, d = x.shape
    assert d == D_MODEL and seq % tile == 0 and tile % (2 * GROUP) == 0 and tile >= HALO

    def resident(shape):
        return pl.BlockSpec(shape, lambda b, t: (0,) * len(shape), pipeline_mode=pl.Buffered(1))

    return pl.pallas_call(
        _block_kernel,
        grid=(batch, seq // tile),
        in_specs=[
            pl.BlockSpec((1, tile, D_MODEL), lambda b, t: (b, t, 0)),
            resident((1, D_MODEL)),
            resident(w_main.shape),
            resident(w_qvt.shape),
            resident(pool_w.shape),
            resident((1, POOL_WIDTH)),
            resident(bias.shape),
            resident(w_out.shape),
            resident((1, D_MODEL)),
        ],
        out_specs=pl.BlockSpec((1, tile, D_MODEL), lambda b, t: (b, t, 0)),
        out_shape=jax.ShapeDtypeStruct(x.shape, x.dtype),
        scratch_shapes=[
            pltpu.VMEM((tile, D_MODEL), _BF16),
            pltpu.VMEM((HALO + tile, ATTN_WIDTH), _BF16),
            pltpu.VMEM(((HALO + tile) // LANES, ATTN_WIDTH, LANES), _BF16),
            pltpu.VMEM((tile // GROUP, ATTN_WIDTH, GROUP), _BF16),
            pltpu.VMEM((tile, ATTN_WIDTH), _F32),
            pltpu.VMEM((tile, POOL_WIDTH), _F32),
            pltpu.VMEM((tile, POOL_WIDTH + ATTN_WIDTH), _BF16),
            pltpu.VMEM((POOL_HALO + tile, POOL_WIDTH), _F32),
            pltpu.VMEM((LOOKAHEAD + 1, GROUP_BAND, SCORE_COLS), _F32),
            pltpu.VMEM((LOOKAHEAD + 1, 1, SCORE_COLS), _F32),
        ],
        compiler_params=pltpu.CompilerParams(
            dimension_semantics=("arbitrary", "arbitrary"),
            vmem_limit_bytes=VMEM_LIMIT_BYTES,
        ),
        name="hybrid_block",
    )(x, g_in, w_main, w_qvt, pool_w, pool_scale, bias, w_out, g_out)


@jax.jit
def kernel(x, norm_gain, w_in, pool_w, pool_scale, rel_bias, w_out, final_norm_gain):
    w_main, w_qvt, w_out_bf, pool_w_bf, bias = _prepare_params(w_in[0], w_out[0], pool_w[0], rel_bias[0])
    return _fused_block(
        x,
        norm_gain[0][None, :],
        w_main,
        w_qvt,
        pool_w_bf,
        pool_scale[0][None, :],
        bias,
        w_out_bf,
        final_norm_gain[None, :],
        tile=SEQ_TILE,
    )
```

```python
import math

import jax
import jax.numpy as jnp
from jax import lax
from jax.experimental import pallas as pl
from jax.experimental.pallas import tpu as pltpu

D_MODEL = 1024
CHUNK = 64
POOL_WIDTH = 512
ATTN_WIDTH = 512
POOL_WINDOWS = (2, 4, 8, 16)
POOL_GROUP_DIM = 128
HEAD_DIM = 64
N_HEADS = 8
LEFT_CHUNKS = 8
HALO = LEFT_CHUNKS * CHUNK
MAX_REL = 64
N_REL = 2 * MAX_REL + 1
EPS = 1e-6
MASK_VALUE = -1e30
LOG2_E = math.log2(math.e)
Q_SCALE = LOG2_E / math.sqrt(HEAD_DIM)

LANES = 128
HEADS_PER_TILE = LANES // HEAD_DIM
N_PAIRS = N_HEADS // HEADS_PER_TILE
GROUP = LANES
GROUP_BAND = HALO + GROUP
SCORE_COLS = HEADS_PER_TILE * GROUP
BIAS_ROWS = GROUP + MAX_REL
SEQ_TILE = 1024
POOL_HALO = 16
OUT_COLS = 256
OUT_ROWS = 256
N_ATTN_PIECES = 4
LOOKAHEAD = 3
VMEM_LIMIT_BYTES = 60 * 1024 * 1024

_F32 = jnp.float32
_BF16 = jnp.bfloat16
_NT = (((1,), (1,)), ((), ()))


def _rms_norm(x, gain):
    return x * lax.rsqrt(jnp.mean(x * x, axis=-1, keepdims=True) + EPS) * gain


def _silu(x):
    return x / (1.0 + jnp.exp(-x))


PREP_ROWS = 128
_POOL_V, _POOL_G, _Q, _K, _V, _ATTN_G = range(6)


def _bias_table(rel_ref, out_ref):
    assert N_REL == LANES + 1
    j = lax.broadcasted_iota(jnp.int32, (BIAS_ROWS, GROUP), 0) + (GROUP_BAND - BIAS_ROWS)
    i = lax.broadcasted_iota(jnp.int32, (BIAS_ROWS, GROUP), 1)
    idx = jnp.clip(j - HALO - i, -MAX_REL, MAX_REL) + MAX_REL
    low = jnp.minimum(idx, LANES - 1)
    for head in range(N_HEADS):
        p, hh = divmod(head, HEADS_PER_TILE)
        row = jnp.broadcast_to(rel_ref[head:head + 1, 0:LANES], (BIAS_ROWS, LANES))
        last = jnp.broadcast_to(rel_ref[head:head + 1, LANES:LANES + 1], (BIAS_ROWS, LANES))
        first = jnp.broadcast_to(rel_ref[head:head + 1, 0:1], (BIAS_ROWS, LANES))
        table = jnp.where(idx == LANES, last, jnp.take_along_axis(row, low, axis=1))
        out_ref[p, :, hh * GROUP:(hh + 1) * GROUP] = (table - first) * LOG2_E


def _prep_kernel(w_in_ref, w_out_ref, pool_w_ref, rel_ref,
                 w_main_ref, w_qvt_ref, w_out_bf_ref, pool_w_bf_ref, bias_ref):
    def cols(part):
        return w_in_ref[:, part * POOL_WIDTH:(part + 1) * POOL_WIDTH]

    w_main_ref[...] = jnp.concatenate([cols(_POOL_V), cols(_POOL_G), cols(_K), cols(_ATTN_G)], axis=1).astype(_BF16)
    w_qvt_ref[...] = jnp.concatenate([cols(_Q), cols(_V)], axis=1).T.astype(_BF16)
    w_out_bf_ref[...] = w_out_ref[...].astype(_BF16)

    @pl.when(pl.program_id(0) == 0)
    def _():
        pool_w_bf_ref[...] = pool_w_ref[...].astype(_BF16)
        _bias_table(rel_ref, bias_ref)


def _prepare_params(w_in, w_out, pool_w, rel_bias):
    n_main = 4 * POOL_WIDTH
    n_qv = 2 * ATTN_WIDTH

    def whole(a):
        return pl.BlockSpec(a.shape, lambda i: (0,) * a.ndim)

    bias_shape = (N_PAIRS, BIAS_ROWS, SCORE_COLS)
    return pl.pallas_call(
        _prep_kernel,
        grid=(D_MODEL // PREP_ROWS,),
        in_specs=[
            pl.BlockSpec((PREP_ROWS, w_in.shape[1]), lambda i: (i, 0)),
            pl.BlockSpec((PREP_ROWS, D_MODEL), lambda i: (i, 0)),
            whole(pool_w),
            whole(rel_bias),
        ],
        out_specs=[
            pl.BlockSpec((PREP_ROWS, n_main), lambda i: (i, 0)),
            pl.BlockSpec((n_qv, PREP_ROWS), lambda i: (0, i)),
            pl.BlockSpec((PREP_ROWS, D_MODEL), lambda i: (i, 0)),
            pl.BlockSpec(pool_w.shape, lambda i: (0, 0, 0)),
            pl.BlockSpec(bias_shape, lambda i: (0, 0, 0)),
        ],
        out_shape=[
            jax.ShapeDtypeStruct((D_MODEL, n_main), _BF16),
            jax.ShapeDtypeStruct((n_qv, D_MODEL), _BF16),
            jax.ShapeDtypeStruct((POOL_WIDTH + ATTN_WIDTH, D_MODEL), _BF16),
            jax.ShapeDtypeStruct(pool_w.shape, _BF16),
            jax.ShapeDtypeStruct(bias_shape, _F32),
        ],
        compiler_params=pltpu.CompilerParams(dimension_semantics=("arbitrary",)),
        name="prepare_params",
    )(w_in, w_out, pool_w, rel_bias)


def _attend(groups, fillers, start_penalty, k_buf, vt_buf, qt_buf, gate_buf, bias_ref, y_buf, s_buf, m_buf):
    row = lax.broadcasted_iota(jnp.int32, (LANES, GROUP), 0)
    first_rows = row < HEAD_DIM
    lane = lax.broadcasted_iota(jnp.int32, (CHUNK, SCORE_COLS), 1)
    second_chunk = (lane & (GROUP - 1)) >= CHUNK
    items = [(g, p) for g in groups for p in range(N_PAIRS)]
    assert len(fillers) == len(items)

    def scores(i):
        g, p = items[i]
        lanes = slice(p * LANES, (p + 1) * LANES)
        band = slice(g * GROUP, g * GROUP + GROUP_BAND)
        lhs = k_buf[band, lanes]
        qt = qt_buf[g, lanes, :]
        zero = jnp.zeros_like(qt)
        rhs = jnp.concatenate([jnp.where(first_rows, qt, zero), jnp.where(first_rows, zero, qt)], axis=1)
        mid = GROUP_BAND // 2
        lo = GROUP_BAND - BIAS_ROWS
        bias = bias_ref[p]
        s0 = jnp.dot(lhs[:mid], rhs, preferred_element_type=_F32)
        s1 = jnp.dot(lhs[mid:], rhs, preferred_element_type=_F32)
        hist = max(HALO - g * GROUP, 0)
        if hist >= mid:
            s0 = s0 + start_penalty
        elif hist > 0:
            s0 = jnp.concatenate([s0[:hist] + start_penalty, s0[hist:]], axis=0)
        if hist > mid:
            s1 = jnp.concatenate([s1[:hist - mid] + start_penalty, s1[hist - mid:]], axis=0)
        s0 = jnp.concatenate([jnp.where(second_chunk, MASK_VALUE, s0[:CHUNK]), s0[CHUNK:]], axis=0)
        s1 = jnp.concatenate([
            s1[:lo - mid],
            s1[lo - mid:-CHUNK] + bias[:BIAS_ROWS - CHUNK],
            jnp.where(second_chunk, s1[-CHUNK:] + bias[BIAS_ROWS - CHUNK:], MASK_VALUE),
        ], axis=0)
        slot = i % (LOOKAHEAD + 1)
        s_buf[slot, 0:mid, :] = s0
        s_buf[slot, mid:, :] = s1
        m_buf[slot] = jnp.maximum(jnp.max(s0, axis=0, keepdims=True), jnp.max(s1, axis=0, keepdims=True))

    def finish(i):
        g, p = items[i]
        lanes = slice(p * LANES, (p + 1) * LANES)
        rows = slice(g * GROUP, (g + 1) * GROUP)
        slot = i % (LOOKAHEAD + 1)
        e = jnp.exp2(s_buf[slot] - m_buf[slot])
        l = jnp.sum(e, axis=0, keepdims=True)
        vt = jnp.concatenate([vt_buf[g + b, lanes, :] for b in range(GROUP_BAND // LANES)], axis=1)
        o = jnp.dot(vt, e.astype(_BF16), preferred_element_type=_F32)
        inv_l = 1.0 / l
        ot = jnp.concatenate([o[:HEAD_DIM, :GROUP] * inv_l[:, :GROUP],
                              o[HEAD_DIM:, GROUP:] * inv_l[:, GROUP:]], axis=0)
        gated = ot.T * gate_buf[rows, lanes]
        y_buf[rows, POOL_WIDTH + p * LANES:POOL_WIDTH + (p + 1) * LANES] = gated.astype(_BF16)

    for i in range(min(LOOKAHEAD, len(items))):
        scores(i)
    for i in range(len(items)):
        for filler in fillers[i]:
            filler()
        if i + LOOKAHEAD < len(items):
            scores(i + LOOKAHEAD)
        finish(i)


def _spread(pieces, n):
    return [pieces[(i * len(pieces)) // n:((i + 1) * len(pieces)) // n] for i in range(n)]


def _block_kernel(x_ref, g_in_ref, w_main_ref, w_qvt_ref, pool_w_ref, pool_scale_ref, bias_ref,
                  w_out_ref, g_out_ref, out_ref,
                  h_buf, k_buf, vt_buf, qt_buf, gate_buf, pgate_buf, y_buf, pv_buf, s_buf, m_buf):
    t = pl.program_id(1)
    tile = x_ref.shape[1]
    half = tile // 2
    n_groups = tile // GROUP
    halo_blocks = HALO // LANES

    @pl.when(t == 0)
    def _():
        pv_buf[0:POOL_HALO, :] = jnp.zeros((POOL_HALO, POOL_WIDTH), _F32)
        k_buf[0:HALO, :] = jnp.zeros((HALO, ATTN_WIDTH), _BF16)
        for b in range(halo_blocks):
            vt_buf[b] = jnp.zeros((ATTN_WIDTH, LANES), _BF16)

    h_buf[...] = _rms_norm(x_ref[0], g_in_ref[...]).astype(_BF16)

    def proj_pieces(hf):
        rows = slice(hf * half, (hf + 1) * half)

        def main(part, rows=rows):
            lo = part * POOL_WIDTH
            return jnp.dot(h_buf[rows, :], w_main_ref[:, lo:lo + POOL_WIDTH], preferred_element_type=_F32)

        def pool_v():
            pv_buf[POOL_HALO + hf * half:POOL_HALO + (hf + 1) * half, :] = main(0)

        def pool_g():
            pgate_buf[rows, :] = _silu(main(1))

        def keys():
            for r0 in range(hf * half, (hf + 1) * half, OUT_ROWS):
                k_buf[HALO + r0:HALO + r0 + OUT_ROWS, :] = main(2, slice(r0, r0 + OUT_ROWS)).astype(_BF16)

        def attn_g():
            gate_buf[rows, :] = _silu(main(3))

        def transposed(part):
            w = w_qvt_ref[part * ATTN_WIDTH:(part + 1) * ATTN_WIDTH, :]
            return lax.dot_general(w, h_buf[rows, :], _NT, preferred_element_type=_F32)

        def queries():
            qt = transposed(0)
            for gg in range(half // GROUP):
                qt_buf[hf * (half // GROUP) + gg] = (qt[:, gg * GROUP:(gg + 1) * GROUP] * Q_SCALE).astype(_BF16)

        def values():
            vt = transposed(1)
            for gg in range(half // GROUP):
                vt_buf[halo_blocks + hf * (half // GROUP) + gg] = vt[:, gg * GROUP:(gg + 1) * GROUP].astype(_BF16)

        def pool(g):
            def run():
                w = POOL_WINDOWS[g]
                lanes = slice(g * POOL_GROUP_DIM, (g + 1) * POOL_GROUP_DIM)
                base = hf * half
                win = pv_buf[base:base + POOL_HALO + half, lanes]
                shift = 1
                while shift < w:
                    win = win + pltpu.roll(win, shift, 0)
                    shift *= 2
                win = win[POOL_HALO:]
                cur = pv_buf[POOL_HALO + base:POOL_HALO + base + half, lanes]
                head = POOL_HALO
                pos = t * tile + base + lax.broadcasted_iota(jnp.int32, (head, 1), 0)
                count = jnp.minimum(pos + 1, w).astype(_F32)
                mean = jnp.concatenate([win[:head] / count, win[head:] * (1.0 / w)], axis=0)
                d = (mean - cur).astype(_BF16)
                yg = jnp.dot(d, pool_w_ref[g], preferred_element_type=_F32)
                y_buf[rows, lanes] = (yg * pool_scale_ref[:, lanes] * pgate_buf[rows, lanes]).astype(_BF16)
            return run

        return [keys, queries, values, attn_g, pool_v, pool_g] + [pool(g) for g in range(len(POOL_WINDOWS))]

    def out_pieces(hf):
        def project(rows, c):
            def run():
                cols = slice(c * OUT_COLS, (c + 1) * OUT_COLS)
                out_ref[0, rows, cols] = x_ref[0, rows, cols] + jnp.dot(
                    y_buf[rows, :], w_out_ref[:, cols], preferred_element_type=_F32)
            return run

        def norm(rows):
            def run():
                out_ref[0, rows, :] = _rms_norm(out_ref[0, rows, :], g_out_ref[...])
            return run

        pieces = []
        for r0 in range(hf * half, (hf + 1) * half, OUT_ROWS):
            rows = slice(r0, r0 + OUT_ROWS)
            pieces += [project(rows, c) for c in range(D_MODEL // OUT_COLS)] + [norm(rows)]
        return pieces

    bufs = (k_buf, vt_buf, qt_buf, gate_buf, bias_ref, y_buf, s_buf, m_buf)
    groups_per_half = half // GROUP
    items_per_half = groups_per_half * N_PAIRS

    for piece in proj_pieces(0):
        piece()
    second = proj_pieces(1)
    start_penalty = jnp.where(t == 0, MASK_VALUE, 0.0).astype(_F32)
    _attend(list(range(n_groups)),
            _spread(second[:N_ATTN_PIECES], items_per_half - LOOKAHEAD + 1) + [[]] * (LOOKAHEAD - 1)
            + _spread(second[N_ATTN_PIECES:] + out_pieces(0), items_per_half), start_penalty, *bufs)
    for piece in out_pieces(1):
        piece()

    k_buf[0:HALO, :] = k_buf[tile:tile + HALO, :]
    for b in range(halo_blocks):
        vt_buf[b] = vt_buf[n_groups + b]
    pv_buf[0:POOL_HALO, :] = pv_buf[tile:tile + POOL_HALO, :]


def _fused_block(x, g_in, w_main, w_qvt, pool_w, pool_scale, bias, w_out, g_out, *, tile):
    batch, seq, d = x.shape
    assert d == D_MODEL and seq % tile == 0 and tile % (2 * GROUP) == 0 and tile >= HALO

    def resident(shape):
        return pl.BlockSpec(shape, lambda b, t: (0,) * len(shape), pipeline_mode=pl.Buffered(1))

    return pl.pallas_call(
        _block_kernel,
        grid=(batch, seq // tile),
        in_specs=[
            pl.BlockSpec((1, tile, D_MODEL), lambda b, t: (b, t, 0)),
            resident((1, D_MODEL)),
            resident(w_main.shape),
            resident(w_qvt.shape),
            resident(pool_w.shape),
            resident((1, POOL_WIDTH)),
            resident(bias.shape),
            resident(w_out.shape),
            resident((1, D_MODEL)),
        ],
        out_specs=pl.BlockSpec((1, tile, D_MODEL), lambda b, t: (b, t, 0)),
        out_shape=jax.ShapeDtypeStruct(x.shape, x.dtype),
        scratch_shapes=[
            pltpu.VMEM((tile, D_MODEL), _BF16),
            pltpu.VMEM((HALO + tile, ATTN_WIDTH), _BF16),
            pltpu.VMEM(((HALO + tile) // LANES, ATTN_WIDTH, LANES), _BF16),
            pltpu.VMEM((tile // GROUP, ATTN_WIDTH, GROUP), _BF16),
            pltpu.VMEM((tile, ATTN_WIDTH), _F32),
            pltpu.VMEM((tile, POOL_WIDTH), _F32),
            pltpu.VMEM((tile, POOL_WIDTH + ATTN_WIDTH), _BF16),
            pltpu.VMEM((POOL_HALO + tile, POOL_WIDTH), _F32),
            pltpu.VMEM((LOOKAHEAD + 1, GROUP_BAND, SCORE_COLS), _F32),
            pltpu.VMEM((LOOKAHEAD + 1, 1, SCORE_COLS), _F32),
        ],
        compiler_params=pltpu.CompilerParams(
            dimension_semantics=("arbitrary", "arbitrary"),
            vmem_limit_bytes=VMEM_LIMIT_BYTES,
        ),
        name="hybrid_block",
    )(x, g_in, w_main, w_qvt, pool_w, pool_scale, bias, w_out, g_out)


@jax.jit
def kernel(x, norm_gain, w_in, pool_w, pool_scale, rel_bias, w_out, final_norm_gain):
    w_main, w_qvt, w_out_bf, pool_w_bf, bias = _prepare_params(w_in[0], w_out[0], pool_w[0], rel_bias[0])
    return _fused_block(
        x,
        norm_gain[0][None, :],
        w_main,
        w_qvt,
        pool_w_bf,
        pool_scale[0][None, :],
        bias,
        w_out_bf,
        final_norm_gain[None, :],
        tile=SEQ_TILE,
    )
```

```python
import math

import jax
import jax.numpy as jnp
from jax import lax
from jax.experimental import pallas as pl
from jax.experimental.pallas import tpu as pltpu

D_MODEL = 1024
CHUNK = 64
POOL_WIDTH = 512
ATTN_WIDTH = 512
POOL_WINDOWS = (2, 4, 8, 16)
POOL_GROUP_DIM = 128
HEAD_DIM = 64
N_HEADS = 8
LEFT_CHUNKS = 8
HALO = LEFT_CHUNKS * CHUNK
MAX_REL = 64
N_REL = 2 * MAX_REL + 1
EPS = 1e-6
MASK_VALUE = -1e30
LOG2_E = math.log2(math.e)
Q_SCALE = LOG2_E / math.sqrt(HEAD_DIM)

LANES = 128
HEADS_PER_TILE = LANES // HEAD_DIM
N_PAIRS = N_HEADS // HEADS_PER_TILE
GROUP = LANES
GROUP_BAND = HALO + GROUP
SCORE_COLS = HEADS_PER_TILE * GROUP
BIAS_ROWS = GROUP + MAX_REL
SEQ_TILE = 1024
POOL_HALO = 16
OUT_COLS = 256
OUT_ROWS = 256
N_ATTN_PIECES = 4
LOOKAHEAD = 3
VMEM_LIMIT_BYTES = 60 * 1024 * 1024

_F32 = jnp.float32
_BF16 = jnp.bfloat16
_NT = (((1,), (1,)), ((), ()))


def _rms_norm(x, gain):
    return x * lax.rsqrt(jnp.mean(x * x, axis=-1, keepdims=True) + EPS) * gain


def _silu(x):
    return x / (1.0 + jnp.exp(-x))


PREP_ROWS = 128
_POOL_V, _POOL_G, _Q, _K, _V, _ATTN_G = range(6)


def _bias_table(rel_ref, out_ref):
    assert N_REL == LANES + 1
    j = lax.broadcasted_iota(jnp.int32, (BIAS_ROWS, GROUP), 0) + (GROUP_BAND - BIAS_ROWS)
    i = lax.broadcasted_iota(jnp.int32, (BIAS_ROWS, GROUP), 1)
    idx = jnp.clip(j - HALO - i, -MAX_REL, MAX_REL) + MAX_REL
    low = jnp.minimum(idx, LANES - 1)
    for head in range(N_HEADS):
        p, hh = divmod(head, HEADS_PER_TILE)
        row = jnp.broadcast_to(rel_ref[head:head + 1, 0:LANES], (BIAS_ROWS, LANES))
        last = jnp.broadcast_to(rel_ref[head:head + 1, LANES:LANES + 1], (BIAS_ROWS, LANES))
        first = jnp.broadcast_to(rel_ref[head:head + 1, 0:1], (BIAS_ROWS, LANES))
        table = jnp.where(idx == LANES, last, jnp.take_along_axis(row, low, axis=1))
        out_ref[p, :, hh * GROUP:(hh + 1) * GROUP] = (table - first) * LOG2_E


def _prep_kernel(w_in_ref, w_out_ref, pool_w_ref, rel_ref,
                 w_main_ref, w_qvt_ref, w_out_bf_ref, pool_w_bf_ref, bias_ref):
    def cols(part):
        return w_in_ref[:, part * POOL_WIDTH:(part + 1) * POOL_WIDTH]

    w_main_ref[...] = jnp.concatenate([cols(_POOL_V), cols(_POOL_G), cols(_K), cols(_ATTN_G)], axis=1).astype(_BF16)
    w_qvt_ref[...] = jnp.concatenate([cols(_Q), cols(_V)], axis=1).T.astype(_BF16)
    w_out_bf_ref[...] = w_out_ref[...].astype(_BF16)

    @pl.when(pl.program_id(0) == 0)
    def _():
        pool_w_bf_ref[...] = pool_w_ref[...].astype(_BF16)
        _bias_table(rel_ref, bias_ref)


def _prepare_params(w_in, w_out, pool_w, rel_bias):
    n_main = 4 * POOL_WIDTH
    n_qv = 2 * ATTN_WIDTH

    def whole(a):
        return pl.BlockSpec(a.shape, lambda i: (0,) * a.ndim)

    bias_shape = (N_PAIRS, BIAS_ROWS, SCORE_COLS)
    return pl.pallas_call(
        _prep_kernel,
        grid=(D_MODEL // PREP_ROWS,),
        in_specs=[
            pl.BlockSpec((PREP_ROWS, w_in.shape[1]), lambda i: (i, 0)),
            pl.BlockSpec((PREP_ROWS, D_MODEL), lambda i: (i, 0)),
            whole(pool_w),
            whole(rel_bias),
        ],
        out_specs=[
            pl.BlockSpec((PREP_ROWS, n_main), lambda i: (i, 0)),
            pl.BlockSpec((n_qv, PREP_ROWS), lambda i: (0, i)),
            pl.BlockSpec((PREP_ROWS, D_MODEL), lambda i: (i, 0)),
            pl.BlockSpec(pool_w.shape, lambda i: (0, 0, 0)),
            pl.BlockSpec(bias_shape, lambda i: (0, 0, 0)),
        ],
        out_shape=[
            jax.ShapeDtypeStruct((D_MODEL, n_main), _BF16),
            jax.ShapeDtypeStruct((n_qv, D_MODEL), _BF16),
            jax.ShapeDtypeStruct((POOL_WIDTH + ATTN_WIDTH, D_MODEL), _BF16),
            jax.ShapeDtypeStruct(pool_w.shape, _BF16),
            jax.ShapeDtypeStruct(bias_shape, _F32),
        ],
        compiler_params=pltpu.CompilerParams(dimension_semantics=("arbitrary",)),
        name="prepare_params",
    )(w_in, w_out, pool_w, rel_bias)


def _attend(groups, fillers, k_buf, ind_buf, vt_buf, qt_buf, gate_buf, bias_ref, y_buf, s_buf, m_buf):
    row = lax.broadcasted_iota(jnp.int32, (LANES, GROUP), 0)
    first_rows = row < HEAD_DIM
    lane = lax.broadcasted_iota(jnp.int32, (CHUNK, SCORE_COLS), 1)
    second_chunk = (lane & (GROUP - 1)) >= CHUNK
    pen_row = lax.broadcasted_iota(jnp.int32, (LANES, SCORE_COLS), 0)
    pen = jnp.where(pen_row == 0, MASK_VALUE, 0.0).astype(_BF16)
    items = [(g, p) for g in groups for p in range(N_PAIRS)]
    assert len(fillers) == len(items)

    def scores(i):
        g, p = items[i]
        lanes = slice(p * LANES, (p + 1) * LANES)
        band = slice(g * GROUP, g * GROUP + GROUP_BAND)
        lhs = k_buf[band, lanes]
        qt = qt_buf[g, lanes, :]
        zero = jnp.zeros_like(qt)
        rhs = jnp.concatenate([jnp.where(first_rows, qt, zero), jnp.where(first_rows, zero, qt)], axis=1)
        if g * GROUP < HALO:
            lhs = jnp.concatenate([lhs, ind_buf[band, :]], axis=1)
            rhs = jnp.concatenate([rhs, pen], axis=0)
        mid = GROUP_BAND // 2
        lo = GROUP_BAND - BIAS_ROWS
        bias = bias_ref[p]
        s0 = jnp.dot(lhs[:mid], rhs, preferred_element_type=_F32)
        s1 = jnp.dot(lhs[mid:], rhs, preferred_element_type=_F32)
        s0 = jnp.concatenate([jnp.where(second_chunk, MASK_VALUE, s0[:CHUNK]), s0[CHUNK:]], axis=0)
        s1 = jnp.concatenate([
            s1[:lo - mid],
            s1[lo - mid:-CHUNK] + bias[:BIAS_ROWS - CHUNK],
            jnp.where(second_chunk, s1[-CHUNK:] + bias[BIAS_ROWS - CHUNK:], MASK_VALUE),
        ], axis=0)
        slot = i % (LOOKAHEAD + 1)
        s_buf[slot, 0:mid, :] = s0
        s_buf[slot, mid:, :] = s1
        m_buf[slot] = jnp.maximum(jnp.max(s0, axis=0, keepdims=True), jnp.max(s1, axis=0, keepdims=True))

    def finish(i):
        g, p = items[i]
        lanes = slice(p * LANES, (p + 1) * LANES)
        rows = slice(g * GROUP, (g + 1) * GROUP)
        slot = i % (LOOKAHEAD + 1)
        e = jnp.exp2(s_buf[slot] - m_buf[slot])
        l = jnp.sum(e, axis=0, keepdims=True)
        vt = jnp.concatenate([vt_buf[g + b, lanes, :] for b in range(GROUP_BAND // LANES)], axis=1)
        o = jnp.dot(vt, e.astype(_BF16), preferred_element_type=_F32)
        inv_l = 1.0 / l
        ot = jnp.concatenate([o[:HEAD_DIM, :GROUP] * inv_l[:, :GROUP],
                              o[HEAD_DIM:, GROUP:] * inv_l[:, GROUP:]], axis=0)
        gated = ot.T * gate_buf[rows, lanes]
        y_buf[rows, POOL_WIDTH + p * LANES:POOL_WIDTH + (p + 1) * LANES] = gated.astype(_BF16)

    for i in range(min(LOOKAHEAD, len(items))):
        scores(i)
    for i in range(len(items)):
        for filler in fillers[i]:
            filler()
        if i + LOOKAHEAD < len(items):
            scores(i + LOOKAHEAD)
        finish(i)


def _spread(pieces, n):
    return [pieces[(i * len(pieces)) // n:((i + 1) * len(pieces)) // n] for i in range(n)]


def _block_kernel(x_ref, g_in_ref, w_main_ref, w_qvt_ref, pool_w_ref, pool_scale_ref, bias_ref,
                  w_out_ref, g_out_ref, out_ref,
                  h_buf, k_buf, ind_buf, vt_buf, qt_buf, gate_buf, pgate_buf, y_buf, pv_buf, s_buf, m_buf):
    t = pl.program_id(1)
    tile = x_ref.shape[1]
    half = tile // 2
    n_groups = tile // GROUP
    halo_blocks = HALO // LANES

    @pl.when(t == 0)
    def _():
        pv_buf[0:POOL_HALO, :] = jnp.zeros((POOL_HALO, POOL_WIDTH), _F32)
        k_buf[0:HALO, :] = jnp.zeros((HALO, ATTN_WIDTH), _BF16)
        for b in range(halo_blocks):
            vt_buf[b] = jnp.zeros((ATTN_WIDTH, LANES), _BF16)
        lane = lax.broadcasted_iota(jnp.int32, (HALO, LANES), 1)
        ind_buf[0:HALO, :] = jnp.where(lane == 0, 1.0, 0.0).astype(_BF16)
        ind_buf[HALO:, :] = jnp.zeros((tile, LANES), _BF16)

    h_buf[...] = _rms_norm(x_ref[0], g_in_ref[...]).astype(_BF16)

    def proj_pieces(hf):
        rows = slice(hf * half, (hf + 1) * half)

        def main(part, rows=rows):
            lo = part * POOL_WIDTH
            return jnp.dot(h_buf[rows, :], w_main_ref[:, lo:lo + POOL_WIDTH], preferred_element_type=_F32)

        def pool_v():
            pv_buf[POOL_HALO + hf * half:POOL_HALO + (hf + 1) * half, :] = main(0)

        def pool_g():
            pgate_buf[rows, :] = _silu(main(1))

        def keys():
            for r0 in range(hf * half, (hf + 1) * half, OUT_ROWS):
                k_buf[HALO + r0:HALO + r0 + OUT_ROWS, :] = main(2, slice(r0, r0 + OUT_ROWS)).astype(_BF16)

        def attn_g():
            gate_buf[rows, :] = _silu(main(3))

        def transposed(part):
            w = w_qvt_ref[part * ATTN_WIDTH:(part + 1) * ATTN_WIDTH, :]
            return lax.dot_general(w, h_buf[rows, :], _NT, preferred_element_type=_F32)

        def queries():
            qt = transposed(0)
            for gg in range(half // GROUP):
                qt_buf[hf * (half // GROUP) + gg] = (qt[:, gg * GROUP:(gg + 1) * GROUP] * Q_SCALE).astype(_BF16)

        def values():
            vt = transposed(1)
            for gg in range(half // GROUP):
                vt_buf[halo_blocks + hf * (half // GROUP) + gg] = vt[:, gg * GROUP:(gg + 1) * GROUP].astype(_BF16)

        def pool(g):
            def run():
                w = POOL_WINDOWS[g]
                lanes = slice(g * POOL_GROUP_DIM, (g + 1) * POOL_GROUP_DIM)
                base = hf * half
                win = pv_buf[base:base + POOL_HALO + half, lanes]
                shift = 1
                while shift < w:
                    win = win + pltpu.roll(win, shift, 0)
                    shift *= 2
                win = win[POOL_HALO:]
                cur = pv_buf[POOL_HALO + base:POOL_HALO + base + half, lanes]
                head = POOL_HALO
                pos = t * tile + base + lax.broadcasted_iota(jnp.int32, (head, 1), 0)
                count = jnp.minimum(pos + 1, w).astype(_F32)
                mean = jnp.concatenate([win[:head] / count, win[head:] * (1.0 / w)], axis=0)
                d = (mean - cur).astype(_BF16)
                yg = jnp.dot(d, pool_w_ref[g], preferred_element_type=_F32)
                y_buf[rows, lanes] = (yg * pool_scale_ref[:, lanes] * pgate_buf[rows, lanes]).astype(_BF16)
            return run

        return [keys, queries, values, attn_g, pool_v, pool_g] + [pool(g) for g in range(len(POOL_WINDOWS))]

    def out_pieces(hf):
        def project(rows, c):
            def run():
                cols = slice(c * OUT_COLS, (c + 1) * OUT_COLS)
                out_ref[0, rows, cols] = x_ref[0, rows, cols] + jnp.dot(
                    y_buf[rows, :], w_out_ref[:, cols], preferred_element_type=_F32)
            return run

        def norm(rows):
            def run():
                out_ref[0, rows, :] = _rms_norm(out_ref[0, rows, :], g_out_ref[...])
            return run

        pieces = []
        for r0 in range(hf * half, (hf + 1) * half, OUT_ROWS):
            rows = slice(r0, r0 + OUT_ROWS)
            pieces += [project(rows, c) for c in range(D_MODEL // OUT_COLS)] + [norm(rows)]
        return pieces

    bufs = (k_buf, ind_buf, vt_buf, qt_buf, gate_buf, bias_ref, y_buf, s_buf, m_buf)
    groups_per_half = half // GROUP
    items_per_half = groups_per_half * N_PAIRS

    for piece in proj_pieces(0):
        piece()
    second = proj_pieces(1)
    _attend(list(range(n_groups)),
            _spread(second[:N_ATTN_PIECES], items_per_half - LOOKAHEAD + 1) + [[]] * (LOOKAHEAD - 1)
            + _spread(second[N_ATTN_PIECES:] + out_pieces(0), items_per_half), *bufs)
    for piece in out_pieces(1):
        piece()

    k_buf[0:HALO, :] = k_buf[tile:tile + HALO, :]
    for b in range(halo_blocks):
        vt_buf[b] = vt_buf[n_groups + b]
    pv_buf[0:POOL_HALO, :] = pv_buf[tile:tile + POOL_HALO, :]
    ind_buf[0:HALO, :] = jnp.zeros((HALO, LANES), _BF16)


def _fused_block(x, g_in, w_main, w_qvt, pool_w, pool_scale, bias, w_out, g_out, *, tile):
    batch, seq, d = x.shape
    assert d == D_MODEL and seq % tile == 0 and tile % (2 * GROUP) == 0 and tile >= HALO

    def resident(shape):
        return pl.BlockSpec(shape, lambda b, t: (0,) * len(shape), pipeline_mode=pl.Buffered(1))

    return pl.pallas_call(
        _block_kernel,
        grid=(batch, seq // tile),
        in_specs=[
            pl.BlockSpec((1, tile, D_MODEL), lambda b, t: (b, t, 0)),
            resident((1, D_MODEL)),
            resident(w_main.shape),
            resident(w_qvt.shape),
            resident(pool_w.shape),
            resident((1, POOL_WIDTH)),
            resident(bias.shape),
            resident(w_out.shape),
            resident((1, D_MODEL)),
        ],
        out_specs=pl.BlockSpec((1, tile, D_MODEL), lambda b, t: (b, t, 0)),
        out_shape=jax.ShapeDtypeStruct(x.shape, x.dtype),
        scratch_shapes=[
            pltpu.VMEM((tile, D_MODEL), _BF16),
            pltpu.VMEM((HALO + tile, ATTN_WIDTH), _BF16),
            pltpu.VMEM((HALO + tile, LANES), _BF16),
            pltpu.VMEM(((HALO + tile) // LANES, ATTN_WIDTH, LANES), _BF16),
            pltpu.VMEM((tile // GROUP, ATTN_WIDTH, GROUP), _BF16),
            pltpu.VMEM((tile, ATTN_WIDTH), _F32),
            pltpu.VMEM((tile, POOL_WIDTH), _F32),
            pltpu.VMEM((tile, POOL_WIDTH + ATTN_WIDTH), _BF16),
            pltpu.VMEM((POOL_HALO + tile, POOL_WIDTH), _F32),
            pltpu.VMEM((LOOKAHEAD + 1, GROUP_BAND, SCORE_COLS), _F32),
            pltpu.VMEM((LOOKAHEAD + 1, 1, SCORE_COLS), _F32),
        ],
        compiler_params=pltpu.CompilerParams(
            dimension_semantics=("arbitrary", "arbitrary"),
            vmem_limit_bytes=VMEM_LIMIT_BYTES,
        ),
        name="hybrid_block",
    )(x, g_in, w_main, w_qvt, pool_w, pool_scale, bias, w_out, g_out)


@jax.jit
def kernel(x, norm_gain, w_in, pool_w, pool_scale, rel_bias, w_out, final_norm_gain):
    w_main, w_qvt, w_out_bf, pool_w_bf, bias = _prepare_params(w_in[0], w_out[0], pool_w[0], rel_bias[0])
    return _fused_block(
        x,
        norm_gain[0][None, :],
        w_main,
        w_qvt,
        pool_w_bf,
        pool_scale[0][None, :],
        bias,
        w_out_bf,
        final_norm_gain[None, :],
        tile=SEQ_TILE,
    )
```
